```python
import jax, jax.numpy as jnp
from jax import lax
import numpy as np

D_MODEL = 1024
BATCH = 8
SEQ = 4096
DEPTH = 2

HEAD_DIM = 64
N_HEADS = D_MODEL // HEAD_DIM
N_MOBA = N_HEADS // 4
N_FOX = (N_HEADS - N_MOBA) // 2
N_DIL = N_HEADS - N_MOBA - N_FOX
N_ALIBI = N_MOBA + N_DIL
MOBA_BLOCK = 256
MOBA_TOPK = 3
MOBA_QCHUNK = 32
FOX_QBLOCK = 128
DIL_PAIRS = ((128, 1), (512, 4), (2048, 16))
DIL_BLOCK = 128
D_FF = 4 * D_MODEL
QKV_COLS = 3 * D_MODEL
FOX_GATE_COLS = N_FOX * HEAD_DIM
N_IN_COLS = QKV_COLS + N_FOX + FOX_GATE_COLS
W_MOBA = N_MOBA * HEAD_DIM
W_FOX = N_FOX * HEAD_DIM
ATTN_SCALE = HEAD_DIM ** -0.5
EPS = 1e-6
NEG_INF = -1e30

kernel_name = "hymba_moba_fox_dilated_block"


def rms_norm(x, gain):
    xf = x.astype(jnp.float32)
    y = xf * lax.rsqrt(jnp.mean(xf * xf, axis=-1, keepdims=True) + EPS)
    return (y * gain.astype(jnp.float32)).astype(x.dtype)


def alibi_slopes():
    return jnp.exp2(-8.0 * jnp.arange(1, N_ALIBI + 1, dtype=jnp.float32) / N_ALIBI)


def moba_attention(q, k, v, slopes):
    b, h, s, dh = q.shape
    n_blk = -(-s // MOBA_BLOCK)
    s_pad = n_blk * MOBA_BLOCK
    pad = ((0, 0), (0, 0), (0, s_pad - s), (0, 0))
    q, k, v = jnp.pad(q, pad), jnp.pad(k, pad), jnp.pad(v, pad)
    k_blocks = k.reshape(b, h, n_blk, MOBA_BLOCK, dh)
    v_blocks = v.reshape(b, h, n_blk, MOBA_BLOCK, dh)
    k_mean = jnp.mean(k_blocks.astype(jnp.float32), axis=3)
    gate = jnp.einsum('bhsd,bhnd->bhsn', q.astype(jnp.float32), k_mean)
    q_blk = jnp.arange(s_pad) // MOBA_BLOCK
    past = jnp.arange(n_blk)[None, :] < q_blk[:, None]
    gate = jnp.where(past, gate, NEG_INF)
    top_k = min(MOBA_TOPK, n_blk)
    _, sel_idx = lax.top_k(gate, top_k)
    sel_ok = sel_idx < q_blk[:, None]
    gather = jax.vmap(jax.vmap(lambda blocks, idx: blocks[idx]))
    offs = jnp.arange(MOBA_BLOCK)

    def chunk(c):
        start = c * MOBA_QCHUNK
        qc = lax.dynamic_slice_in_dim(q, start, MOBA_QCHUNK, axis=2)
        idx = lax.dynamic_slice_in_dim(sel_idx, start, MOBA_QCHUNK, axis=2)
        ok = lax.dynamic_slice_in_dim(sel_ok, start, MOBA_QCHUNK, axis=2)
        q_pos = start + jnp.arange(MOBA_QCHUNK)
        k_sel = gather(k_blocks, idx)
        v_sel = gather(v_blocks, idx)
        d_sel = (q_pos[:, None, None] - (idx[..., None] * MOBA_BLOCK + offs)).astype(jnp.float32)
        l_sel = (jnp.einsum('bhqd,bhqnkd->bhqnk', qc, k_sel).astype(jnp.float32) * ATTN_SCALE
                 - slopes[:, None, None, None] * d_sel)
        l_sel = jnp.where(ok[..., None], l_sel, NEG_INF).reshape(b, h, MOBA_QCHUNK, top_k * MOBA_BLOCK)
        own_start = (start // MOBA_BLOCK) * MOBA_BLOCK
        k_own = lax.dynamic_slice_in_dim(k, own_start, MOBA_BLOCK, axis=2)
        v_own = lax.dynamic_slice_in_dim(v, own_start, MOBA_BLOCK, axis=2)
        d_own = q_pos[:, None] - (own_start + offs)[None, :]
        l_own = (jnp.einsum('bhqd,bhkd->bhqk', qc, k_own).astype(jnp.float32) * ATTN_SCALE
                 - slopes[:, None, None] * d_own.astype(jnp.float32))
        l_own = jnp.where(d_own >= 0, l_own, NEG_INF)
        p = jax.nn.softmax(jnp.concatenate([l_sel, l_own], axis=-1), axis=-1).astype(v.dtype)
        p_sel = p[..., :top_k * MOBA_BLOCK].reshape(b, h, MOBA_QCHUNK, top_k, MOBA_BLOCK)
        return (jnp.einsum('bhqnk,bhqnkd->bhqd', p_sel, v_sel)
                + jnp.einsum('bhqk,bhkd->bhqd', p[..., top_k * MOBA_BLOCK:], v_own))

    out = lax.map(chunk, jnp.arange(s_pad // MOBA_QCHUNK))
    return out.transpose(1, 2, 0, 3, 4).reshape(b, h, s_pad, dh)[:, :, :s]


def forgetting_attention(q, k, v, f_logit):
    b, h, s, dh = q.shape
    cum = jnp.cumsum(jax.nn.log_sigmoid(f_logit.astype(jnp.float32)), axis=1).transpose(0, 2, 1)
    k_pos = jnp.arange(s)

    def block(i):
        start = i * FOX_QBLOCK
        qb = lax.dynamic_slice_in_dim(q, start, FOX_QBLOCK, axis=2)
        cb = lax.dynamic_slice_in_dim(cum, start, FOX_QBLOCK, axis=2)
        q_pos = start + jnp.arange(FOX_QBLOCK)
        logits = (jnp.einsum('bhqd,bhkd->bhqk', qb, k).astype(jnp.float32) * ATTN_SCALE
                  + cb[..., None] - cum[:, :, None, :])
        logits = jnp.where(k_pos[None, :] <= q_pos[:, None], logits, NEG_INF)
        p = jax.nn.softmax(logits, axis=-1).astype(v.dtype)
        return jnp.einsum('bhqk,bhkd->bhqd', p, v)

    out = lax.map(block, jnp.arange(s // FOX_QBLOCK))
    return out.transpose(1, 2, 0, 3, 4).reshape(b, h, s, dh)


def dilated_branch(q, k, v, slopes, window, dilation):
    b, h, s, dh = q.shape
    steps = window // dilation
    sub_len = s // dilation
    n_blk = -(-sub_len // DIL_BLOCK)
    sub_pad = n_blk * DIL_BLOCK

    def to_sub(t):
        t = t.reshape(b, h, sub_len, dilation, dh).transpose(0, 1, 3, 2, 4)
        t = jnp.pad(t, ((0, 0), (0, 0), (0, 0), (0, sub_pad - sub_len), (0, 0)))
        return t.reshape(b, h, dilation, n_blk, DIL_BLOCK, dh)

    def band(t):
        prev = jnp.concatenate([jnp.zeros_like(t[:, :, :, :1]), t[:, :, :, :-1]], axis=3)
        return jnp.concatenate([prev, t], axis=4)

    qs = to_sub(q)
    kb, vb = band(to_sub(k)), band(to_sub(v))
    blk = jnp.arange(n_blk)[:, None]
    i_q = blk * DIL_BLOCK + jnp.arange(DIL_BLOCK)[None, :]
    i_k = (blk - 1) * DIL_BLOCK + jnp.arange(2 * DIL_BLOCK)[None, :]
    off = i_q[:, :, None] - i_k[:, None, :]
    allowed = (off >= 0) & (off <= steps) & (i_k[:, None, :] >= 0) & (i_k[:, None, :] < sub_len)
    dist = (off * dilation).astype(jnp.float32)
    logits = (jnp.einsum('bhrnqd,bhrnkd->bhrnqk', qs, kb).astype(jnp.float32) * ATTN_SCALE
              - slopes[:, None, None, None, None] * dist)
    logits = jnp.where(allowed, logits, NEG_INF)
    lse = jax.nn.logsumexp(logits, axis=-1)
    p = jnp.exp(logits - lse[..., None]).astype(v.dtype)
    o = jnp.einsum('bhrnqk,bhrnkd->bhrnqd', p, vb)
    o = o.reshape(b, h, dilation, sub_pad, dh)[:, :, :, :sub_len].transpose(0, 1, 3, 2, 4).reshape(b, h, s, dh)
    lse = lse.reshape(b, h, dilation, sub_pad)[..., :sub_len].transpose(0, 1, 3, 2).reshape(b, h, s)
    return o, lse


def dilated_attention(q, k, v, slopes):
    outs, lses = [], []
    for window, dilation in DIL_PAIRS:
        o, lse = dilated_branch(q, k, v, slopes, window, dilation)
        outs.append(o)
        lses.append(lse)
    w = jax.nn.softmax(jnp.stack(lses, axis=0), axis=0)
    o = jnp.sum(w[..., None] * jnp.stack(outs, axis=0).astype(jnp.float32), axis=0)
    return o.astype(q.dtype)


def to_bsd(o):
    b, h, s, dh = o.shape
    return o.transpose(0, 2, 1, 3).reshape(b, s, h * dh)


def mixer_sublayer(x, norm_g, w_in, b_f, q_gain, k_gain, out_gain, w_out):
    b, s, _ = x.shape
    hn = rms_norm(x, norm_g)
    proj = hn @ w_in
    q = proj[..., :D_MODEL].reshape(b, s, N_HEADS, HEAD_DIM)
    k = proj[..., D_MODEL:2 * D_MODEL].reshape(b, s, N_HEADS, HEAD_DIM)
    v = proj[..., 2 * D_MODEL:QKV_COLS].reshape(b, s, N_HEADS, HEAD_DIM)
    f_logit = proj[..., QKV_COLS:QKV_COLS + N_FOX] + b_f
    g_fox = proj[..., QKV_COLS + N_FOX:]
    q = rms_norm(q, q_gain).transpose(0, 2, 1, 3)
    k = rms_norm(k, k_gain).transpose(0, 2, 1, 3)
    v = v.transpose(0, 2, 1, 3)
    slopes = alibi_slopes()
    a0, a1 = N_MOBA, N_MOBA + N_FOX
    o_moba = moba_attention(q[:, :a0], k[:, :a0], v[:, :a0], slopes[N_DIL:])
    o_fox = forgetting_attention(q[:, a0:a1], k[:, a0:a1], v[:, a0:a1], f_logit)
    o_dil = dilated_attention(q[:, a1:], k[:, a1:], v[:, a1:], slopes[:N_DIL])
    y_moba = rms_norm(to_bsd(o_moba), out_gain[:W_MOBA])
    y_fox = rms_norm(to_bsd(o_fox), out_gain[W_MOBA:W_MOBA + W_FOX]) * jax.nn.sigmoid(g_fox)
    y_dil = rms_norm(to_bsd(o_dil), out_gain[W_MOBA + W_FOX:])
    y = jnp.concatenate([y_moba, y_fox, y_dil], axis=-1) @ w_out
    return x + y


def mlp_sublayer(x, norm_g, w_up, w_down):
    hn = rms_norm(x, norm_g)
    return x + jnp.square(jax.nn.relu(hn @ w_up)) @ w_down


def setup_inputs(seed: int = 0) -> dict:
    key = jax.random.key(seed)
    ks = jax.random.split(key, 11)
    f32 = jnp.float32
    return {
        "x": jax.random.normal(ks[0], (BATCH, SEQ, D_MODEL), f32),
        "attn_norm": 1.0 + 0.02 * jax.random.normal(ks[1], (DEPTH, D_MODEL), f32),
        "w_in": jax.random.normal(ks[2], (DEPTH, D_MODEL, N_IN_COLS), f32) * D_MODEL ** -0.5,
        "b_forget": 2.0 + 0.5 * jax.random.normal(ks[3], (DEPTH, N_FOX), f32),
        "q_gain": 1.0 + 0.02 * jax.random.normal(ks[4], (DEPTH, N_HEADS, HEAD_DIM), f32),
        "k_gain": 1.0 + 0.02 * jax.random.normal(ks[5], (DEPTH, N_HEADS, HEAD_DIM), f32),
        "out_gain": 1.0 + 0.02 * jax.random.normal(ks[6], (DEPTH, D_MODEL), f32),
        "w_out": jax.random.normal(ks[7], (DEPTH, D_MODEL, D_MODEL), f32) * D_MODEL ** -0.5,
        "mlp_norm": 1.0 + 0.02 * jax.random.normal(ks[8], (DEPTH, D_MODEL), f32),
        "w_up": jax.random.normal(ks[9], (DEPTH, D_MODEL, D_FF), f32) * D_MODEL ** -0.5,
        "w_down": jax.random.normal(ks[10], (DEPTH, D_FF, D_MODEL), f32) * D_FF ** -0.5,
    }


def reference(x, attn_norm, w_in, b_forget, q_gain, k_gain, out_gain, w_out, mlp_norm, w_up, w_down):
    for l in range(DEPTH):
        x = mixer_sublayer(x, attn_norm[l], w_in[l], b_forget[l], q_gain[l], k_gain[l],
                           out_gain[l], w_out[l])
        x = mlp_sublayer(x, mlp_norm[l], w_up[l], w_down[l])
    return x
```

```python
import functools

import jax
import jax.numpy as jnp
from jax import lax
from jax.experimental import pallas as pl
from jax.experimental.pallas import tpu as pltpu

D_MODEL = 1024
HEAD_DIM = 64
N_HEADS = D_MODEL // HEAD_DIM
N_MOBA = N_HEADS // 4
N_FOX = (N_HEADS - N_MOBA) // 2
N_DIL = N_HEADS - N_MOBA - N_FOX
N_ALIBI = N_MOBA + N_DIL
MOBA_BLOCK = 256
MOBA_TOPK = 3
DIL_PAIRS = ((128, 1), (512, 4), (2048, 16))
DIL_BLOCK = 128
D_FF = 4 * D_MODEL
QKV_COLS = 3 * D_MODEL
W_MOBA = N_MOBA * HEAD_DIM
W_FOX = N_FOX * HEAD_DIM
W_DIL = N_DIL * HEAD_DIM
ATTN_SCALE = HEAD_DIM ** -0.5
EPS = 1e-6
NEG_INF = -1e30

LANES = 128
AUX_COLS = 512
F_COL_BLOCK = W_FOX // LANES
VMEM_LIMIT = 56 * 1024 * 1024

ROW_TILE = 512
QK_CHUNK = 512
ATT_TILE = 256
SCAN_TILE = 128

f32 = jnp.float32
bf16 = jnp.bfloat16


def _cparams(n_axes):
    return pltpu.CompilerParams(
        dimension_semantics=("arbitrary",) * n_axes,
        vmem_limit_bytes=VMEM_LIMIT,
    )


def _dot_nt(a, b):
    return lax.dot_general(a, b, (((1,), (1,)), ((), ())), preferred_element_type=f32)


def _lane_iota(shape):
    return lax.broadcasted_iota(jnp.int32, shape, len(shape) - 1)


def _inproj_kernel(x_ref, g_ref, wqkv_ref, waux_ref, baux_ref, gain_ref, gmat_ref,
                   q_ref, k_ref, v_ref, aux_ref):
    x = x_ref[...]
    ms = jnp.mean(x * x, axis=-1, keepdims=True)
    hn = (x * lax.rsqrt(ms + EPS) * g_ref[...]).astype(bf16)
    n_chunks = QKV_COLS // QK_CHUNK
    per = D_MODEL // QK_CHUNK
    for c in range(n_chunks):
        cols = slice(c * QK_CHUNK, (c + 1) * QK_CHUNK)
        acc = jnp.dot(hn, wqkv_ref[:, cols], preferred_element_type=f32)
        dst = (q_ref, k_ref, v_ref)[c // per]
        dcols = slice((c % per) * QK_CHUNK, (c % per + 1) * QK_CHUNK)
        if c // per < 2:
            msq = jnp.dot((acc * acc).astype(bf16), gmat_ref[...], preferred_element_type=f32)
            acc = acc * lax.rsqrt(msq + EPS) * gain_ref[:, cols]
        dst[:, dcols] = acc.astype(bf16)
    aux_ref[...] = jnp.dot(hn, waux_ref[...], preferred_element_type=f32) + baux_ref[...]


def _inproj(x2d, g, wqkv, waux, baux, gain, gmat):
    m = x2d.shape[0]
    row = lambda i: (i, 0)
    const = lambda i: (0, 0)
    return pl.pallas_call(
        _inproj_kernel,
        grid=(m // ROW_TILE,),
        in_specs=[
            pl.BlockSpec((ROW_TILE, D_MODEL), row),
            pl.BlockSpec((1, D_MODEL), const),
            pl.BlockSpec((D_MODEL, QKV_COLS), const),
            pl.BlockSpec((D_MODEL, AUX_COLS), const),
            pl.BlockSpec((1, AUX_COLS), const),
            pl.BlockSpec((1, 2 * D_MODEL), const),
            pl.BlockSpec((QK_CHUNK, QK_CHUNK), const),
        ],
        out_specs=[
            pl.BlockSpec((ROW_TILE, D_MODEL), row),
            pl.BlockSpec((ROW_TILE, D_MODEL), row),
            pl.BlockSpec((ROW_TILE, D_MODEL), row),
            pl.BlockSpec((ROW_TILE, AUX_COLS), row),
        ],
        out_shape=[
            jax.ShapeDtypeStruct((m, D_MODEL), bf16),
            jax.ShapeDtypeStruct((m, D_MODEL), bf16),
            jax.ShapeDtypeStruct((m, D_MODEL), bf16),
            jax.ShapeDtypeStruct((m, AUX_COLS), f32),
        ],
        compiler_params=_cparams(1),
        name="inproj",
    )(x2d, g, wqkv, waux, baux, gain, gmat)


def _scan_kernel(f_ref, ccol_ref, crow_ref):
    n = f_ref.shape[1] // SCAN_TILE
    r = lax.broadcasted_iota(jnp.int32, (SCAN_TILE, SCAN_TILE), 0)
    c = lax.broadcasted_iota(jnp.int32, (SCAN_TILE, SCAN_TILE), 1)
    tri = jnp.where(r >= c, 1.0, 0.0).astype(f32)

    def body(i, carry):
        rows = pl.ds(pl.multiple_of(i * SCAN_TILE, SCAN_TILE), SCAN_TILE)
        f = f_ref[0, rows, :]
        ls = jnp.minimum(f, 0.0) - jnp.log(1.0 + jnp.exp(-jnp.abs(f)))
        cum = jnp.dot(tri, ls, preferred_element_type=f32,
                      precision=lax.Precision.HIGHEST) + carry
        ccol_ref[0, rows, :] = cum
        crow_ref[0, i] = cum.T[0:8, :]
        return cum[SCAN_TILE - 1:SCAN_TILE, :]

    lax.fori_loop(0, n, body, jnp.zeros((1, LANES), f32))


def _fox_scan(aux3):
    b, s, _ = aux3.shape
    return pl.pallas_call(
        _scan_kernel,
        grid=(b,),
        in_specs=[pl.BlockSpec((1, s, LANES), lambda i: (i, 0, F_COL_BLOCK))],
        out_specs=[
            pl.BlockSpec((1, s, LANES), lambda i: (i, 0, 0)),
            pl.BlockSpec((1, s // SCAN_TILE, 8, SCAN_TILE), lambda i: (i, 0, 0, 0)),
        ],
        out_shape=[
            jax.ShapeDtypeStruct((b, s, LANES), f32),
            jax.ShapeDtypeStruct((b, s // SCAN_TILE, 8, SCAN_TILE), f32),
        ],
        compiler_params=_cparams(1),
        name="fox_scan",
    )(aux3)


def _softmax_step(s, v_aug, m_ref, acc_ref, h):
    m_prev = m_ref[h]
    m_new = jnp.maximum(m_prev, jnp.max(s, axis=-1, keepdims=True))
    alpha = jnp.exp(m_prev - m_new)
    p = jnp.exp(s - m_new).astype(bf16)
    acc_ref[h] = alpha * acc_ref[h] + jnp.dot(p, v_aug, preferred_element_type=f32)
    m_ref[h] = m_new


def _split_heads(x2):
    first = _lane_iota(x2.shape) < HEAD_DIM
    zero = jnp.zeros_like(x2)
    return jnp.where(first, x2, zero), jnp.where(first, zero, x2)


def _ones_on_other_head(v2):
    first = _lane_iota(v2.shape) < HEAD_DIM
    one = jnp.ones_like(v2)
    return jnp.where(first, v2, one), jnp.where(first, one, v2)


def _finish_pair(acc_ref):
    acc_a = acc_ref[0]
    acc_b = acc_ref[1]
    o_a = acc_a / pltpu.roll(acc_a, HEAD_DIM, 1)
    o_b = acc_b / pltpu.roll(acc_b, HEAD_DIM, 1)
    return jnp.where(_lane_iota(acc_a.shape) < HEAD_DIM, o_a, o_b)


def _fox_kernel(q_ref, k_ref, v_ref, ccol_ref, crow_ref, o_ref, m_ref, acc_ref):
    p = pl.program_id(1)
    i = pl.program_id(2)
    t = ATT_TILE
    qa, qb = _split_heads(q_ref[0])
    lane = _lane_iota((t, LANES))
    ccol = ccol_ref[0]
    cq = [jnp.sum(jnp.where(lane == 2 * p + h, ccol, 0.0), axis=-1, keepdims=True)
          for h in range(2)]
    m_ref[...] = jnp.full(m_ref.shape, NEG_INF, f32)
    acc_ref[...] = jnp.zeros(acc_ref.shape, f32)

    def tile(j, causal):
        rows = pl.ds(pl.multiple_of(j * t, t), t)
        k2 = k_ref[0, rows, :]
        va, vb = _ones_on_other_head(v_ref[0, rows, :])
        for h, (qh, vh) in enumerate(((qa, va), (qb, vb))):
            ck = jnp.concatenate(
                [crow_ref[0, 2 * j, pl.ds(2 * p + h, 1), :],
                 crow_ref[0, 2 * j + 1, pl.ds(2 * p + h, 1), :]], axis=1)
            s = _dot_nt(qh, k2) + (cq[h] - ck)
            if causal:
                r = lax.broadcasted_iota(jnp.int32, (t, t), 0)
                c = lax.broadcasted_iota(jnp.int32, (t, t), 1)
                s = jnp.where(c <= r, s, NEG_INF)
            _softmax_step(s, vh, m_ref, acc_ref, h)

    tile(i, True)

    def body(j, carry):
        tile(j, False)
        return carry

    lax.fori_loop(0, i, body, 0)
    o_ref[0] = _finish_pair(acc_ref)


def _fox_attention(q, k, v, ccol, crow):
    b, s, _ = q.shape
    t = ATT_TILE
    base = W_MOBA // LANES
    return pl.pallas_call(
        _fox_kernel,
        grid=(b, W_FOX // LANES, s // t),
        in_specs=[
            pl.BlockSpec((1, t, LANES), lambda bi, p, i: (bi, i, base + p)),
            pl.BlockSpec((1, s, LANES), lambda bi, p, i: (bi, 0, base + p)),
            pl.BlockSpec((1, s, LANES), lambda bi, p, i: (bi, 0, base + p)),
            pl.BlockSpec((1, t, LANES), lambda bi, p, i: (bi, i, 0)),
            pl.BlockSpec((1, s // SCAN_TILE, 8, SCAN_TILE), lambda bi, p, i: (bi, 0, 0, 0)),
        ],
        out_specs=pl.BlockSpec((1, t, LANES), lambda bi, p, i: (bi, i, p)),
        out_shape=jax.ShapeDtypeStruct((b, s, W_FOX), f32),
        scratch_shapes=[
            pltpu.VMEM((2, t, 1), f32),
            pltpu.VMEM((2, t, LANES), f32),
        ],
        compiler_params=_cparams(3),
        name="fox_attention",
    )(q, k, v, ccol, crow)


def _moba_kernel(slope_ref, q_ref, k_ref, v_ref, o_ref, kmean_ref, bias_ref, m_ref, acc_ref):
    p = pl.program_id(1)
    i = pl.program_id(2)
    t = MOBA_BLOCK
    n_blk = k_ref.shape[1] // t

    @pl.when(i == 0)
    def _():
        kmean_ref[...] = jnp.zeros(kmean_ref.shape, f32)
        for blk in range(n_blk):
            kb = k_ref[0, blk * t:(blk + 1) * t, :].astype(f32)
            kmean_ref[blk:blk + 1, :] = jnp.mean(kb, axis=0, keepdims=True)

    q2 = q_ref[0]
    q_heads = _split_heads(q2)
    lane = _lane_iota((t, LANES))
    row = lax.broadcasted_iota(jnp.int32, (t, 1), 0)
    q_pos = (i * t + row).astype(f32)
    slopes = [slope_ref[2 * p + h] for h in range(2)]

    for h in range(2):
        gate = lax.dot_general(q_heads[h].astype(f32), kmean_ref[...],
                               (((1,), (1,)), ((), ())), preferred_element_type=f32,
                               precision=lax.Precision.HIGHEST)
        rank = jnp.zeros((t, LANES), f32)
        for j in range(n_blk):
            gj = gate[:, j:j + 1]
            beats = (gj > gate) | ((gj == gate) & (j < lane))
            rank = rank + jnp.where(j < i, jnp.where(beats, 1.0, 0.0), 0.0)
        keep = jnp.where((rank < MOBA_TOPK) & (lane < i), 1.0, 0.0)
        for j in range(n_blk):
            @pl.when(j < i)
            def _(j=j, h=h, keep=keep):
                dist = q_pos - float(j * t)
                col = jnp.where(keep[:, j:j + 1] > 0.5, -slopes[h] * dist, NEG_INF)
                bias_ref[h, j] = jnp.broadcast_to(col, (t, LANES))

    m_ref[...] = jnp.full(m_ref.shape, NEG_INF, f32)
    acc_ref[...] = jnp.zeros(acc_ref.shape, f32)
    k_off = lax.broadcasted_iota(jnp.int32, (1, t), 1).astype(f32)

    def tile(j, own):
        rows = pl.ds(pl.multiple_of(j * t, t), t)
        k2 = k_ref[0, rows, :]
        v_aug = _ones_on_other_head(v_ref[0, rows, :])
        for h in range(2):
            s = _dot_nt(q_heads[h], k2)
            if own:
                r = lax.broadcasted_iota(jnp.int32, (t, t), 0)
                c = lax.broadcasted_iota(jnp.int32, (t, t), 1)
                s = s - slopes[h] * (r - c).astype(f32)
                s = jnp.where(c <= r, s, NEG_INF)
            else:
                bias = bias_ref[h, j]
                s = s + jnp.concatenate([bias] * (t // LANES), axis=1) + slopes[h] * k_off
            _softmax_step(s, v_aug[h], m_ref, acc_ref, h)

    tile(i, True)

    def body(j, carry):
        tile(j, False)
        return carry

    lax.fori_loop(0, i, body, 0)
    o_ref[0] = _finish_pair(acc_ref)


def _moba_attention(slopes, q, k, v):
    b, s, _ = q.shape
    t = MOBA_BLOCK
    n_blk = s // t
    return pl.pallas_call(
        _moba_kernel,
        grid=(b, W_MOBA // LANES, n_blk),
        in_specs=[
            pl.BlockSpec(memory_space=pltpu.SMEM),
            pl.BlockSpec((1, t, LANES), lambda bi, p, i: (bi, i, p)),
            pl.BlockSpec((1, s, LANES), lambda bi, p, i: (bi, 0, p)),
            pl.BlockSpec((1, s, LANES), lambda bi, p, i: (bi, 0, p)),
        ],
        out_specs=pl.BlockSpec((1, t, LANES), lambda bi, p, i: (bi, i, p)),
        out_shape=jax.ShapeDtypeStruct((b, s, W_MOBA), f32),
        scratch_shapes=[
            pltpu.VMEM((LANES, LANES), f32),
            pltpu.VMEM((2, n_blk, t, LANES), f32),
            pltpu.VMEM((2, t, 1), f32),
            pltpu.VMEM((2, t, LANES), f32),
        ],
        compiler_params=_cparams(3),
        name="moba_attention",
    )(slopes, q, k, v)


def _dilated_kernel(slope_ref, q_ref, k_ref, v_ref, o_ref,
                    stage_ref, qs_ref, ks_ref, vs_ref, part_ref):
    p = pl.program_id(1)
    s_len = q_ref.shape[1]
    blk = DIL_BLOCK
    n_blk = s_len // blk
    slopes = [slope_ref[2 * p + h] for h in range(2)]
    n_br = len(DIL_PAIRS)

    zeros_blk = jnp.zeros((blk, LANES), bf16)
    for src_ref, dst_ref, pad in ((q_ref, qs_ref, 0), (k_ref, ks_ref, blk), (v_ref, vs_ref, blk)):
        stage_ref[...] = src_ref[0].astype(f32)
        for g, (_, d) in enumerate(DIL_PAIRS):
            if pad:
                dst_ref[g, 0:blk, :] = zeros_blk
            sub = s_len // d
            for r in range(d):
                rows = stage_ref[pl.ds(r, sub, stride=d), :] if d > 1 else stage_ref[...]
                dst_ref[g, pad + r * sub:pad + (r + 1) * sub, :] = rows.astype(bf16)

    iq = lax.broadcasted_iota(jnp.int32, (blk, 2 * blk), 0) + blk
    ik = lax.broadcasted_iota(jnp.int32, (blk, 2 * blk), 1)
    off = iq - ik
    lane = _lane_iota((blk, LANES))
    for g, (window, d) in enumerate(DIL_PAIRS):
        steps = window // d
        per_res = n_blk // d
        in_band = (off >= 0) & (off <= steps)
        dist = (off * d).astype(f32)

        def block(nb, carry, g=g, d=d, per_res=per_res, in_band=in_band, dist=dist):
            res = nb // per_res
            n = nb - res * per_res
            q2 = qs_ref[g, pl.ds(pl.multiple_of(nb * blk, blk), blk), :]
            keys = pl.ds(pl.multiple_of(nb * blk, blk), 2 * blk)
            k2 = ks_ref[g, keys, :]
            v_aug = _ones_on_other_head(vs_ref[g, keys, :])
            q_heads = _split_heads(q2)
            first_key = jnp.where(n > 0, 0, blk)
            allowed = in_band & (ik >= first_key)
            out_rows = pl.ds(res + n * (blk * d), blk, stride=d) if d > 1 else \
                pl.ds(pl.multiple_of(nb * blk, blk), blk)
            for h in range(2):
                s = _dot_nt(q_heads[h], k2) - slopes[h] * dist
                s = jnp.where(allowed, s, NEG_INF)
                m = jnp.max(s, axis=-1, keepdims=True)
                pr = jnp.exp(s - m).astype(bf16)
                acc = jnp.dot(pr, v_aug[h], preferred_element_type=f32)
                stash = LANES - 1 if h == 0 else 0
                acc = jnp.where(lane == stash, m, acc)
                part_ref[g * 2 + h, out_rows, :] = acc
            return carry

        lax.fori_loop(0, n_blk, block, 0)

    t = ATT_TILE

    def merge(i, carry):
        rows = pl.ds(pl.multiple_of(i * t, t), t)
        lane_t = _lane_iota((t, LANES))
        outs = []
        for h in range(2):
            stash = LANES - 1 if h == 0 else 0
            denom = HEAD_DIM if h == 0 else 1
            parts = [part_ref[g * 2 + h, rows, :] for g in range(n_br)]
            maxes = [x[:, stash:stash + 1] for x in parts]
            m_all = functools.reduce(jnp.maximum, maxes)
            tot = sum(jnp.exp(mg - m_all) * x for mg, x in zip(maxes, parts))
            outs.append(tot / tot[:, denom:denom + 1])
        o_ref[0, rows, :] = jnp.where(lane_t < HEAD_DIM, outs[0], outs[1])
        return carry

    lax.fori_loop(0, s_len // t, merge, 0)


def _dilated_attention(slopes, q, k, v):
    b, s, _ = q.shape
    base = (W_MOBA + W_FOX) // LANES
    n_br = len(DIL_PAIRS)
    spec = pl.BlockSpec((1, s, LANES), lambda bi, p: (bi, 0, base + p))
    return pl.pallas_call(
        _dilated_kernel,
        grid=(b, W_DIL // LANES),
        in_specs=[pl.BlockSpec(memory_space=pltpu.SMEM), spec, spec, spec],
        out_specs=pl.BlockSpec((1, s, LANES), lambda bi, p: (bi, 0, p)),
        out_shape=jax.ShapeDtypeStruct((b, s, W_DIL), f32),
        scratch_shapes=[
            pltpu.VMEM((s, LANES), f32),
            pltpu.VMEM((n_br, s, LANES), bf16),
            pltpu.VMEM((n_br, s + DIL_BLOCK, LANES), bf16),
            pltpu.VMEM((n_br, s + DIL_BLOCK, LANES), bf16),
            pltpu.VMEM((n_br * 2, s, LANES), f32),
        ],
        compiler_params=_cparams(2),
        name="dilated_attention",
    )(slopes, q, k, v)


def _rms(o, gain):
    return o * lax.rsqrt(jnp.mean(o * o, axis=-1, keepdims=True) + EPS) * gain


def _outproj_kernel(x_ref, om_ref, of_ref, od_ref, gf_ref, gain_ref, w_ref, out_ref, y_ref):
    a0, a1 = W_MOBA, W_MOBA + W_FOX
    y_ref[:, 0:a0] = _rms(om_ref[...], gain_ref[:, 0:a0]).astype(bf16)
    y_fox = _rms(of_ref[...], gain_ref[:, a0:a1]) * jax.nn.sigmoid(gf_ref[...])
    y_ref[:, a0:a1] = y_fox.astype(bf16)
    y_ref[:, a1:] = _rms(od_ref[...], gain_ref[:, a1:]).astype(bf16)
    out_ref[...] = x_ref[...] + jnp.dot(y_ref[...], w_ref[...], preferred_element_type=f32)


def _outproj(x2d, o_moba, o_fox, o_dil, aux, gain, w_out):
    m = x2d.shape[0]
    row = lambda i: (i, 0)
    const = lambda i: (0, 0)
    return pl.pallas_call(
        _outproj_kernel,
        grid=(m // ROW_TILE,),
        in_specs=[
            pl.BlockSpec((ROW_TILE, D_MODEL), row),
            pl.BlockSpec((ROW_TILE, W_MOBA), row),
            pl.BlockSpec((ROW_TILE, W_FOX), row),
            pl.BlockSpec((ROW_TILE, W_DIL), row),
            pl.BlockSpec((ROW_TILE, W_FOX), row),
            pl.BlockSpec((1, D_MODEL), const),
            pl.BlockSpec((D_MODEL, D_MODEL), const),
        ],
        out_specs=pl.BlockSpec((ROW_TILE, D_MODEL), row),
        out_shape=jax.ShapeDtypeStruct((m, D_MODEL), f32),
        scratch_shapes=[pltpu.VMEM((ROW_TILE, D_MODEL), bf16)],
        compiler_params=_cparams(1),
        name="outproj",
    )(x2d, o_moba, o_fox, o_dil, aux, gain, w_out)


def _mlp_kernel(x_ref, g_ref, wup_ref, wdown_ref, out_ref, acc_ref):
    x = x_ref[...]
    hn = _rms(x, g_ref[...]).astype(bf16)
    acc_ref[...] = x
    for c in range(D_FF // QK_CHUNK):
        cols = slice(c * QK_CHUNK, (c + 1) * QK_CHUNK)
        h = jnp.maximum(jnp.dot(hn, wup_ref[:, cols], preferred_element_type=f32), 0.0)
        acc_ref[...] += jnp.dot((h * h).astype(bf16), wdown_ref[cols, :],
                                preferred_element_type=f32)
    out_ref[...] = acc_ref[...]


def _mlp(x2d, g, w_up, w_down):
    m = x2d.shape[0]
    row = lambda i: (i, 0)
    const = lambda i: (0, 0)
    return pl.pallas_call(
        _mlp_kernel,
        grid=(m // ROW_TILE,),
        in_specs=[
            pl.BlockSpec((ROW_TILE, D_MODEL), row),
            pl.BlockSpec((1, D_MODEL), const),
            pl.BlockSpec((D_MODEL, D_FF), const),
            pl.BlockSpec((D_FF, D_MODEL), const),
        ],
        out_specs=pl.BlockSpec((ROW_TILE, D_MODEL), row),
        out_shape=jax.ShapeDtypeStruct((m, D_MODEL), f32),
        scratch_shapes=[pltpu.VMEM((ROW_TILE, D_MODEL), f32)],
        compiler_params=_cparams(1),
        name="mlp",
    )(x2d, g, w_up, w_down)


def _alibi_slopes():
    return jnp.exp2(-8.0 * jnp.arange(1, N_ALIBI + 1, dtype=f32) / N_ALIBI)


def _head_mean_matrix():
    r = jnp.arange(QK_CHUNK)[:, None] // HEAD_DIM
    c = jnp.arange(QK_CHUNK)[None, :] // HEAD_DIM
    return jnp.where(r == c, 1.0 / HEAD_DIM, 0.0).astype(bf16)


def kernel(x, attn_norm, w_in, b_forget, q_gain, k_gain, out_gain, w_out, mlp_norm, w_up, w_down):
    b, s, d = x.shape
    assert d == D_MODEL and s % (DIL_BLOCK * DIL_PAIRS[-1][1]) == 0 and (b * s) % ROW_TILE == 0
    depth = w_in.shape[0]
    slopes = _alibi_slopes()
    gmat = _head_mean_matrix()
    x2d = x.reshape(b * s, d)
    for l in range(depth):
        wqkv = w_in[l][:, :QKV_COLS].astype(bf16)
        pad = jnp.zeros((d, AUX_COLS - W_FOX - N_FOX), f32)
        waux = jnp.concatenate(
            [w_in[l][:, QKV_COLS + N_FOX:], w_in[l][:, QKV_COLS:QKV_COLS + N_FOX], pad],
            axis=1).astype(bf16)
        baux = jnp.zeros((1, AUX_COLS), f32).at[0, W_FOX:W_FOX + N_FOX].set(b_forget[l])
        gain = jnp.concatenate(
            [q_gain[l].reshape(1, d) * ATTN_SCALE, k_gain[l].reshape(1, d)], axis=1)
        q, k, v, aux = _inproj(x2d, attn_norm[l].reshape(1, d), wqkv, waux, baux, gain, gmat)
        q3, k3, v3 = (a.reshape(b, s, d) for a in (q, k, v))
        ccol, crow = _fox_scan(aux.reshape(b, s, AUX_COLS))
        o_moba = _moba_attention(slopes[N_DIL:], q3, k3, v3)
        o_fox = _fox_attention(q3, k3, v3, ccol, crow)
        o_dil = _dilated_attention(slopes[:N_DIL], q3, k3, v3)
        x2d = _outproj(x2d, o_moba.reshape(b * s, W_MOBA), o_fox.reshape(b * s, W_FOX),
                       o_dil.reshape(b * s, W_DIL), aux, out_gain[l].reshape(1, d),
                       w_out[l].astype(bf16))
        x2d = _mlp(x2d, mlp_norm[l].reshape(1, d), w_up[l].astype(bf16), w_down[l].astype(bf16))
    return x2d.reshape(b, s, d)
```

```python
import functools
import math

import jax
import jax.numpy as jnp
from jax import lax
from jax.experimental import pallas as pl
from jax.experimental.pallas import tpu as pltpu

D_MODEL = 1024
HEAD_DIM = 64
N_HEADS = D_MODEL // HEAD_DIM
N_MOBA = N_HEADS // 4
N_FOX = (N_HEADS - N_MOBA) // 2
N_DIL = N_HEADS - N_MOBA - N_FOX
N_ALIBI = N_MOBA + N_DIL
MOBA_BLOCK = 256
MOBA_TOPK = 3
DIL_PAIRS = ((128, 1), (512, 4), (2048, 16))
DIL_BLOCK = 128
D_FF = 4 * D_MODEL
QKV_COLS = 3 * D_MODEL
W_MOBA = N_MOBA * HEAD_DIM
W_FOX = N_FOX * HEAD_DIM
W_DIL = N_DIL * HEAD_DIM
ATTN_SCALE = HEAD_DIM ** -0.5
EPS = 1e-6
NEG_INF = -1e30
LOG2E = math.log2(math.e)

LANES = 128
AUX_COLS = 512
F_COL_BLOCK = AUX_COLS // LANES - 1
F_LANE0 = LANES - N_FOX
VMEM_LIMIT = 56 * 1024 * 1024

ROW_TILE = 512
QK_CHUNK = 512
ATT_TILE = 512
SCAN_TILE = 128
DIL_UNROLL = 4
MERGE_TILE = 256

OFF_FOX, OFF_DIL, OFF_MOBA = 0, W_FOX, W_FOX + W_DIL
MASK_VAL = -(2.0 ** 100)
N_SPLIT = 3

f32 = jnp.float32
bf16 = jnp.bfloat16


def _cparams(n_axes):
    return pltpu.CompilerParams(
        dimension_semantics=("arbitrary",) * n_axes,
        vmem_limit_bytes=VMEM_LIMIT,
    )


def _dot_nt(a, b):
    return lax.dot_general(a, b, (((1,), (1,)), ((), ())), preferred_element_type=f32)


def _lane_iota(shape):
    return lax.broadcasted_iota(jnp.int32, shape, len(shape) - 1)


def _split3(x):
    hi = x.astype(bf16).astype(f32)
    r = x - hi
    mid = r.astype(bf16).astype(f32)
    lo = (r - mid).astype(bf16).astype(f32)
    return hi, mid, lo


def _own_lanes(lane, h):
    return (lane < HEAD_DIM) if h == 0 else (lane >= HEAD_DIM)


def _spare_base(h):
    return HEAD_DIM if h == 0 else 0


def _cast_kernel(w_ref, o_ref):
    o_ref[...] = w_ref[0].astype(bf16)


def _cast_aux_kernel(w_ref, o_ref):
    col = _lane_iota((D_MODEL, AUX_COLS))
    w = jnp.where(col < N_FOX + W_FOX, w_ref[0], 0.0)
    o_ref[...] = pltpu.roll(w, AUX_COLS - N_FOX, 1).astype(bf16)


def _inproj_weights(w_in, l):
    d = D_MODEL
    n_lane_blocks = d // LANES
    moba_blocks = W_MOBA // LANES

    def src_block(j):
        return (j // n_lane_blocks) * n_lane_blocks + (j % n_lane_blocks + moba_blocks) % n_lane_blocks

    wqkv = pl.pallas_call(
        _cast_kernel,
        grid=(QKV_COLS // LANES,),
        in_specs=[pl.BlockSpec((1, d, LANES), lambda j: (l, 0, src_block(j)))],
        out_specs=pl.BlockSpec((d, LANES), lambda j: (0, j)),
        out_shape=jax.ShapeDtypeStruct((d, QKV_COLS), bf16),
        compiler_params=_cparams(1),
        name="cast_qkv",
    )(w_in)
    waux = pl.pallas_call(
        _cast_aux_kernel,
        grid=(1,),
        in_specs=[pl.BlockSpec((1, d, AUX_COLS), lambda j: (l, 0, QKV_COLS // AUX_COLS))],
        out_specs=pl.BlockSpec((d, AUX_COLS), lambda j: (0, 0)),
        out_shape=jax.ShapeDtypeStruct((d, AUX_COLS), bf16),
        compiler_params=_cparams(1),
        name="cast_aux",
    )(w_in)
    return wqkv, waux


def _inproj_kernel(x_ref, g_ref, wqkv_ref, waux_ref, baux_ref, gain_ref, gmat_ref,
                   q_ref, k_ref, v_ref, aux_ref):
    x = x_ref[...]
    ms = jnp.mean(x * x, axis=-1, keepdims=True)
    hn = (x * lax.rsqrt(ms + EPS) * g_ref[...]).astype(bf16)
    n_chunks = QKV_COLS // QK_CHUNK
    per = D_MODEL // QK_CHUNK
    for c in range(n_chunks):
        cols = slice(c * QK_CHUNK, (c + 1) * QK_CHUNK)
        acc = jnp.dot(hn, wqkv_ref[:, cols], preferred_element_type=f32)
        dst = (q_ref, k_ref, v_ref)[c // per]
        dcols = slice((c % per) * QK_CHUNK, (c % per + 1) * QK_CHUNK)
        if c // per < 2:
            msq = jnp.dot((acc * acc).astype(bf16), gmat_ref[...], preferred_element_type=f32)
            acc = acc * lax.rsqrt(msq + EPS) * gain_ref[:, cols]
        dst[:, dcols] = acc.astype(bf16)
    aux_ref[...] = jnp.dot(hn, waux_ref[...], preferred_element_type=f32) + baux_ref[...]


def _inproj(x2d, g, wqkv, waux, baux, gain, gmat):
    m = x2d.shape[0]
    row = lambda i: (i, 0)
    const = lambda i: (0, 0)
    return pl.pallas_call(
        _inproj_kernel,
        grid=(m // ROW_TILE,),
        in_specs=[
            pl.BlockSpec((ROW_TILE, D_MODEL), row),
            pl.BlockSpec((1, D_MODEL), const),
            pl.BlockSpec((D_MODEL, QKV_COLS), const),
            pl.BlockSpec((D_MODEL, AUX_COLS), const),
            pl.BlockSpec((1, AUX_COLS), const),
            pl.BlockSpec((1, 2 * D_MODEL), const),
            pl.BlockSpec((QK_CHUNK, QK_CHUNK), const),
        ],
        out_specs=[
            pl.BlockSpec((ROW_TILE, D_MODEL), row),
            pl.BlockSpec((ROW_TILE, D_MODEL), row),
            pl.BlockSpec((ROW_TILE, D_MODEL), row),
            pl.BlockSpec((ROW_TILE, AUX_COLS), row),
        ],
        out_shape=[
            jax.ShapeDtypeStruct((m, D_MODEL), bf16),
            jax.ShapeDtypeStruct((m, D_MODEL), bf16),
            jax.ShapeDtypeStruct((m, D_MODEL), bf16),
            jax.ShapeDtypeStruct((m, AUX_COLS), f32),
        ],
        compiler_params=_cparams(1),
        name="inproj",
    )(x2d, g, wqkv, waux, baux, gain, gmat)


def _scan_kernel(f_ref, c_ref):
    n = f_ref.shape[1] // SCAN_TILE
    r = lax.broadcasted_iota(jnp.int32, (SCAN_TILE, SCAN_TILE), 0)
    c = lax.broadcasted_iota(jnp.int32, (SCAN_TILE, SCAN_TILE), 1)
    tri = jnp.where(r >= c, 1.0, 0.0).astype(f32)

    def body(i, carry):
        rows = pl.ds(pl.multiple_of(i * SCAN_TILE, SCAN_TILE), SCAN_TILE)
        f = f_ref[0, rows, :]
        ls = jnp.minimum(f, 0.0) - jnp.log(1.0 + jnp.exp(-jnp.abs(f)))
        cum = jnp.dot(tri, ls, preferred_element_type=f32,
                      precision=lax.Precision.HIGHEST) + carry
        c_ref[0, rows, :] = cum * LOG2E
        return cum[SCAN_TILE - 1:SCAN_TILE, :]

    lax.fori_loop(0, n, body, jnp.zeros((1, LANES), f32))


def _fox_scan(aux3):
    b, s, _ = aux3.shape
    return pl.pallas_call(
        _scan_kernel,
        grid=(b,),
        in_specs=[pl.BlockSpec((1, s, LANES), lambda i: (i, 0, F_COL_BLOCK))],
        out_specs=pl.BlockSpec((1, s, LANES), lambda i: (i, 0, 0)),
        out_shape=jax.ShapeDtypeStruct((b, s, LANES), f32),
        compiler_params=_cparams(1),
        name="fox_scan",
    )(aux3)


def _softmax_step(s, v_aug, m_ref, acc_ref, head):
    rows = s.shape[0]
    m_prev = m_ref[head]
    row_max = jnp.max(s, axis=-1, keepdims=True)
    m_new = jnp.maximum(m_prev, jnp.broadcast_to(row_max, (rows, LANES)))
    alpha = jnp.exp2(m_prev - m_new)
    m_wide = jnp.concatenate([m_new] * (s.shape[1] // LANES), axis=1)
    p = jnp.exp2(s - m_wide).astype(bf16)
    acc_ref[head] = alpha * acc_ref[head] + jnp.dot(p, v_aug, preferred_element_type=f32)
    m_ref[head] = m_new


def _ones_on_other_head(v2):
    first = _lane_iota(v2.shape) < HEAD_DIM
    one = jnp.ones_like(v2)
    return jnp.where(first, v2, one), jnp.where(first, one, v2)


def _finish_pair(acc_a, acc_b):
    o_a = acc_a / pltpu.roll(acc_a, HEAD_DIM, 1)
    o_b = acc_b / pltpu.roll(acc_b, HEAD_DIM, 1)
    return jnp.where(_lane_iota(acc_a.shape) < HEAD_DIM, o_a, o_b)


def _causal_chunk_then_past(i, n_heads, scores, values, m_ref, acc_ref, o_ref):
    t = ATT_TILE
    m_ref[...] = jnp.full(m_ref.shape, NEG_INF, f32)
    acc_ref[...] = jnp.zeros(acc_ref.shape, f32)

    def chunk(j, causal):
        rows = pl.ds(pl.multiple_of(j * t, t), t)
        for p in range(n_heads // 2):
            v_aug = _ones_on_other_head(values(rows, p))
            for h in range(2):
                head = 2 * p + h
                s = scores(head, rows)
                if causal:
                    r = lax.broadcasted_iota(jnp.int32, (t, t), 0)
                    c = lax.broadcasted_iota(jnp.int32, (t, t), 1)
                    s = jnp.where(c <= r, s, NEG_INF)
                _softmax_step(s, v_aug[h], m_ref, acc_ref, head)

    chunk(i, True)

    def body(j, carry):
        chunk(j, False)
        return carry

    lax.fori_loop(0, i, body, 0)
    for p in range(n_heads // 2):
        o_ref[0, :, p * LANES:(p + 1) * LANES] = _finish_pair(acc_ref[2 * p], acc_ref[2 * p + 1])


def _fox_kernel(q_ref, k_ref, v_ref, c_ref, o_ref, kaug_ref, m_ref, acc_ref):
    i = pl.program_id(1)
    t = ATT_TILE
    s_len = k_ref.shape[1]
    lane = _lane_iota((t, LANES))

    @pl.when(i == 0)
    def _():
        def build(r, carry):
            rows = pl.ds(pl.multiple_of(r * t, t), t)
            c = c_ref[0, rows, :]
            for p in range(N_FOX // 2):
                k2 = k_ref[0, rows, p * LANES:(p + 1) * LANES].astype(f32)
                for h in range(2):
                    head = 2 * p + h
                    base = _spare_base(h)
                    fl = F_LANE0 + head
                    pieces = _split3(-jnp.broadcast_to(c[:, fl:fl + 1], (t, LANES)))
                    ka = jnp.where(_own_lanes(lane, h), k2, 0.0)
                    for n, piece in enumerate(pieces):
                        ka = jnp.where(lane == base + n, piece, ka)
                    kaug_ref[head, rows, :] = ka.astype(bf16)
            return carry
        lax.fori_loop(0, s_len // t, build, 0)

    q_heads = []
    for p in range(N_FOX // 2):
        q2 = q_ref[0, :, p * LANES:(p + 1) * LANES].astype(f32)
        for h in range(2):
            base = _spare_base(h)
            ones = jnp.where((lane >= base) & (lane < base + N_SPLIT), 1.0, 0.0)
            q_heads.append(jnp.where(_own_lanes(lane, h), q2, ones).astype(bf16))

    _causal_chunk_then_past(
        i, N_FOX,
        lambda head, rows: _dot_nt(q_heads[head], kaug_ref[head, rows, :]),
        lambda rows, p: v_ref[0, rows, p * LANES:(p + 1) * LANES],
        m_ref, acc_ref, o_ref)


def _fox_attention(q, k, v, cum):
    b, s, _ = q.shape
    t = ATT_TILE
    blk = OFF_FOX // W_FOX
    return pl.pallas_call(
        _fox_kernel,
        grid=(b, s // t),
        in_specs=[
            pl.BlockSpec((1, t, W_FOX), lambda bi, i: (bi, i, blk)),
            pl.BlockSpec((1, s, W_FOX), lambda bi, i: (bi, 0, blk)),
            pl.BlockSpec((1, s, W_FOX), lambda bi, i: (bi, 0, blk)),
            pl.BlockSpec((1, s, LANES), lambda bi, i: (bi, 0, 0)),
        ],
        out_specs=pl.BlockSpec((1, t, W_FOX), lambda bi, i: (bi, i, 0)),
        out_shape=jax.ShapeDtypeStruct((b, s, W_FOX), f32),
        scratch_shapes=[
            pltpu.VMEM((N_FOX, s, LANES), bf16),
            pltpu.VMEM((N_FOX, t, LANES), f32),
            pltpu.VMEM((N_FOX, t, LANES), f32),
        ],
        compiler_params=_cparams(2),
        name="fox_attention",
    )(q, k, v, cum)


def _moba_kernel(qc_ref, q_ref, k_ref, v_ref, o_ref, kaug_ref, kmean_ref, m_ref, acc_ref):
    i = pl.program_id(1)
    t = ATT_TILE
    bs = MOBA_BLOCK
    s_len = k_ref.shape[1]
    n_blk = s_len // bs
    per_tile = t // bs
    lane = _lane_iota((t, LANES))
    mask_lane0 = 2 * N_SPLIT

    @pl.when(i == 0)
    def _():
        kmean_ref[...] = jnp.zeros(kmean_ref.shape, f32)

        def build(r, carry):
            rows = pl.ds(pl.multiple_of(r * t, t), t)
            pos = r * t + lax.broadcasted_iota(jnp.int32, (t, LANES), 0)
            blk = pos // bs
            in_blk = (pos - blk * bs).astype(f32)
            blk_start = (blk * bs).astype(f32)
            for p in range(N_MOBA // 2):
                k2 = k_ref[0, rows, p * LANES:(p + 1) * LANES].astype(f32)
                for u in range(per_tile):
                    kmean_ref[p, pl.ds(r * per_tile + u, 1), :] = jnp.mean(
                        k2[u * bs:(u + 1) * bs], axis=0, keepdims=True)
                for h in range(2):
                    base = _spare_base(h)
                    rel = lane - base
                    ka = jnp.where(_own_lanes(lane, h), k2, 0.0)
                    ka = jnp.where((rel >= 0) & (rel < N_SPLIT), in_blk, ka)
                    ka = jnp.where((rel >= N_SPLIT) & (rel < mask_lane0), blk_start, ka)
                    ka = jnp.where(rel - mask_lane0 == blk, MASK_VAL, ka)
                    kaug_ref[2 * p + h, rows, :] = ka.astype(bf16)
            return carry
        lax.fori_loop(0, s_len // t, build, 0)

    sub = lax.broadcasted_iota(jnp.int32, (n_blk, t), 0)
    q_blk = i * per_tile + lax.broadcasted_iota(jnp.int32, (1, t), 1) // bs
    q_heads = []
    for p in range(N_MOBA // 2):
        q2 = q_ref[0, :, p * LANES:(p + 1) * LANES].astype(f32)
        for h in range(2):
            head = 2 * p + h
            q_own = jnp.where(_own_lanes(lane, h), q2, 0.0)
            gate = lax.dot_general(kmean_ref[p, 0:n_blk, :], q_own, (((1,), (1,)), ((), ())),
                                   preferred_element_type=f32,
                                   precision=lax.Precision.HIGHEST)
            rank = jnp.zeros((n_blk, t), f32)
            for j in range(n_blk):
                gj = gate[j:j + 1, :]
                beats = (gj > gate) | ((gj == gate) & (j < sub))
                rank = rank + jnp.where(beats & (j < q_blk), 1.0, 0.0)
            dropped = jnp.where((sub < q_blk) & (rank >= MOBA_TOPK), 1.0, 0.0)
            dropped = jnp.concatenate([dropped, jnp.zeros((LANES - n_blk, t), f32)], axis=0)
            dropped = pltpu.roll(dropped.T, _spare_base(h) + mask_lane0, 1)
            q_heads.append((q_own + qc_ref[head:head + 1, :] + dropped).astype(bf16))

    _causal_chunk_then_past(
        i, N_MOBA,
        lambda head, rows: _dot_nt(q_heads[head], kaug_ref[head, rows, :]),
        lambda rows, p: v_ref[0, rows, p * LANES:(p + 1) * LANES],
        m_ref, acc_ref, o_ref)


def _moba_attention(q_const, q, k, v):
    b, s, _ = q.shape
    t = ATT_TILE
    blk = OFF_MOBA // W_MOBA
    assert 2 * N_SPLIT + s // MOBA_BLOCK <= HEAD_DIM and s // MOBA_BLOCK <= LANES
    return pl.pallas_call(
        _moba_kernel,
        grid=(b, s // t),
        in_specs=[
            pl.BlockSpec((8, LANES), lambda bi, i: (0, 0)),
            pl.BlockSpec((1, t, W_MOBA), lambda bi, i: (bi, i, blk)),
            pl.BlockSpec((1, s, W_MOBA), lambda bi, i: (bi, 0, blk)),
            pl.BlockSpec((1, s, W_MOBA), lambda bi, i: (bi, 0, blk)),
        ],
        out_specs=pl.BlockSpec((1, t, W_MOBA), lambda bi, i: (bi, i, 0)),
        out_shape=jax.ShapeDtypeStruct((b, s, W_MOBA), f32),
        scratch_shapes=[
            pltpu.VMEM((N_MOBA, s, LANES), bf16),
            pltpu.VMEM((N_MOBA // 2, LANES, LANES), f32),
            pltpu.VMEM((N_MOBA, t, LANES), f32),
            pltpu.VMEM((N_MOBA, t, LANES), f32),
        ],
        compiler_params=_cparams(2),
        name="moba_attention",
    )(q_const, q, k, v)


def _alibi_query_lanes(slopes):
    n = slopes.shape[0]
    pieces = _split3(slopes * LOG2E)
    lanes = jnp.arange(LANES)
    rows = []
    for head in range(n):
        rel = lanes - _spare_base(head % 2)
        row = jnp.zeros((LANES,), f32)
        for j in range(2 * N_SPLIT):
            row = jnp.where(rel == j, pieces[j % N_SPLIT][head], row)
        rows.append(row)
    return jnp.stack(rows)


def _dilated_kernel(qc_ref, q_ref, k_ref, v_ref, o_ref,
                    stage_ref, qs_ref, ks_ref, vs_ref, part_ref):
    s_len = q_ref.shape[1]
    blk = DIL_BLOCK
    n_blk = s_len // blk
    n_br = len(DIL_PAIRS)
    ch = 2 * blk

    for idx, src_ref in enumerate((q_ref, k_ref, v_ref)):
        stage_ref[idx] = src_ref[0].astype(f32)
    zeros_blk = jnp.zeros((blk, LANES), bf16)
    lane = _lane_iota((ch, LANES))
    for g, (_, d) in enumerate(DIL_PAIRS):
        vs_ref[g, 0:blk, :] = zeros_blk
        ks_ref[2 * g, 0:blk, :] = zeros_blk
        ks_ref[2 * g + 1, 0:blk, :] = zeros_blk
        per_res = s_len // d // ch

        def gather(u, carry, g=g, d=d, per_res=per_res):
            res = u // per_res
            sub0 = (u - res * per_res) * ch
            if d > 1:
                src = pl.ds(res + sub0 * d, ch, stride=d)
            else:
                src = pl.ds(pl.multiple_of(u * ch, ch), ch)
            dst = pl.ds(pl.multiple_of(u * ch, ch), ch)
            pdst = pl.ds(pl.multiple_of(u * ch + blk, blk), ch)
            q2, k2, v2 = stage_ref[0, src, :], stage_ref[1, src, :], stage_ref[2, src, :]
            pos = res + d * (sub0 + lax.broadcasted_iota(jnp.int32, (ch, LANES), 0))
            pos_hi = (pos // blk) * blk
            pos_lo = (pos - pos_hi).astype(f32)
            pos_hi = pos_hi.astype(f32)
            vs_ref[g, pdst, :] = v2.astype(bf16)
            for h in range(2):
                own = _own_lanes(lane, h)
                rel = lane - _spare_base(h)
                qa = jnp.where(own, q2, 0.0) + qc_ref[0, h:h + 1, :]
                qs_ref[2 * g + h, dst, :] = qa.astype(bf16)
                ka = jnp.where(own, k2, 0.0)
                ka = jnp.where((rel >= 0) & (rel < N_SPLIT), pos_lo, ka)
                ka = jnp.where((rel >= N_SPLIT) & (rel < 2 * N_SPLIT), pos_hi, ka)
                ks_ref[2 * g + h, pdst, :] = ka.astype(bf16)
            return carry

        lax.fori_loop(0, s_len // ch, gather, 0)

    iq = lax.broadcasted_iota(jnp.int32, (blk, 2 * blk), 0) + blk
    ik = lax.broadcasted_iota(jnp.int32, (blk, 2 * blk), 1)
    off = iq - ik
    lane_b = _lane_iota((blk, LANES))
    for g, (window, d) in enumerate(DIL_PAIRS):
        steps = window // d
        per_res = n_blk // d
        in_band = (off >= 0) & (off <= steps)

        def blocks(it, carry, g=g, d=d, per_res=per_res, in_band=in_band):
            for u in range(DIL_UNROLL):
                nb = it * DIL_UNROLL + u
                res = nb // per_res
                n = nb - res * per_res
                q_rows = pl.ds(pl.multiple_of(nb * blk, blk), blk)
                keys = pl.ds(pl.multiple_of(nb * blk, blk), 2 * blk)
                v_aug = _ones_on_other_head(vs_ref[g, keys, :])
                first_key = jnp.where(n > 0, 0, blk)
                allowed = in_band & (ik >= first_key)
                if d > 1:
                    out_rows = pl.ds(res + n * (blk * d), blk, stride=d)
                else:
                    out_rows = q_rows
                for h in range(2):
                    s = _dot_nt(qs_ref[2 * g + h, q_rows, :], ks_ref[2 * g + h, keys, :])
                    s = jnp.where(allowed, s, NEG_INF)
                    m = jnp.broadcast_to(jnp.max(s, axis=-1, keepdims=True), (blk, LANES))
                    pr = jnp.exp2(s - jnp.concatenate([m, m], axis=1)).astype(bf16)
                    acc = jnp.dot(pr, v_aug[h], preferred_element_type=f32)
                    stash = LANES - 1 if h == 0 else 0
                    part_ref[2 * g + h, out_rows, :] = jnp.where(lane_b == stash, m, acc)
            return carry

        lax.fori_loop(0, n_blk // DIL_UNROLL, blocks, 0)

    t = MERGE_TILE

    def merge(i, carry):
        rows = pl.ds(pl.multiple_of(i * t, t), t)
        outs = []
        for h in range(2):
            stash = LANES - 1 if h == 0 else 0
            denom = HEAD_DIM if h == 0 else 1
            parts = [part_ref[2 * g + h, rows, :] for g in range(n_br)]
            maxes = [x[:, stash:stash + 1] for x in parts]
            m_all = functools.reduce(jnp.maximum, maxes)
            tot = sum(jnp.exp2(mg - m_all) * x for mg, x in zip(maxes, parts))
            outs.append(tot / tot[:, denom:denom + 1])
        o_ref[0, rows, :] = jnp.where(_lane_iota((t, LANES)) < HEAD_DIM, outs[0], outs[1])
        return carry

    lax.fori_loop(0, s_len // t, merge, 0)


def _dilated_attention(q_const, q, k, v):
    b, s, _ = q.shape
    base = OFF_DIL // LANES
    n_br = len(DIL_PAIRS)
    assert s % (2 * DIL_BLOCK * DIL_PAIRS[-1][1]) == 0 and (s // DIL_BLOCK) % DIL_UNROLL == 0
    spec = pl.BlockSpec((1, s, LANES), lambda bi, p: (bi, 0, base + p))
    return pl.pallas_call(
        _dilated_kernel,
        grid=(b, W_DIL // LANES),
        in_specs=[pl.BlockSpec((1, 8, LANES), lambda bi, p: (p, 0, 0)), spec, spec, spec],
        out_specs=pl.BlockSpec((1, s, LANES), lambda bi, p: (bi, 0, p)),
        out_shape=jax.ShapeDtypeStruct((b, s, W_DIL), f32),
        scratch_shapes=[
            pltpu.VMEM((3, s, LANES), f32),
            pltpu.VMEM((2 * n_br, s, LANES), bf16),
            pltpu.VMEM((2 * n_br, s + DIL_BLOCK, LANES), bf16),
            pltpu.VMEM((n_br, s + DIL_BLOCK, LANES), bf16),
            pltpu.VMEM((2 * n_br, s, LANES), f32),
        ],
        compiler_params=_cparams(2),
        name="dilated_attention",
    )(q_const, q, k, v)


def _dilated_query_lanes(slopes):
    rows = _alibi_query_lanes(slopes)
    pairs = rows.reshape(N_DIL // 2, 2, LANES)
    return jnp.concatenate([pairs, jnp.zeros((N_DIL // 2, 6, LANES), f32)], axis=1)


def _rms(o, gain):
    return o * lax.rsqrt(jnp.mean(o * o, axis=-1, keepdims=True) + EPS) * gain


def _outproj_kernel(x_ref, of_ref, od_ref, om_ref, gf_ref, gain_ref, w_ref, out_ref, y_ref):
    a0, a1 = OFF_DIL, OFF_MOBA
    y_fox = _rms(of_ref[...], gain_ref[:, 0:a0]) * jax.nn.sigmoid(gf_ref[...])
    y_ref[:, 0:a0] = y_fox.astype(bf16)
    y_ref[:, a0:a1] = _rms(od_ref[...], gain_ref[:, a0:a1]).astype(bf16)
    y_ref[:, a1:] = _rms(om_ref[...], gain_ref[:, a1:]).astype(bf16)
    out_ref[...] = x_ref[...] + jnp.dot(y_ref[...], w_ref[...], preferred_element_type=f32)


def _outproj(x2d, o_fox, o_dil, o_moba, aux, gain, w_out):
    m = x2d.shape[0]
    row = lambda i: (i, 0)
    const = lambda i: (0, 0)
    return pl.pallas_call(
        _outproj_kernel,
        grid=(m // ROW_TILE,),
        in_specs=[
            pl.BlockSpec((ROW_TILE, D_MODEL), row),
            pl.BlockSpec((ROW_TILE, W_FOX), row),
            pl.BlockSpec((ROW_TILE, W_DIL), row),
            pl.BlockSpec((ROW_TILE, W_MOBA), row),
            pl.BlockSpec((ROW_TILE, W_FOX), row),
            pl.BlockSpec((1, D_MODEL), const),
            pl.BlockSpec((D_MODEL, D_MODEL), const),
        ],
        out_specs=pl.BlockSpec((ROW_TILE, D_MODEL), row),
        out_shape=jax.ShapeDtypeStruct((m, D_MODEL), f32),
        scratch_shapes=[pltpu.VMEM((ROW_TILE, D_MODEL), bf16)],
        compiler_params=_cparams(1),
        name="outproj",
    )(x2d, o_fox, o_dil, o_moba, aux, gain, w_out)


def _mlp_kernel(x_ref, g_ref, wup_ref, wdown_ref, out_ref, acc_ref):
    x = x_ref[...]
    hn = _rms(x, g_ref[...]).astype(bf16)
    acc_ref[...] = x
    for c in range(D_FF // QK_CHUNK):
        cols = slice(c * QK_CHUNK, (c + 1) * QK_CHUNK)
        h = jnp.maximum(jnp.dot(hn, wup_ref[:, cols], preferred_element_type=f32), 0.0)
        acc_ref[...] += jnp.dot((h * h).astype(bf16), wdown_ref[cols, :],
                                preferred_element_type=f32)
    out_ref[...] = acc_ref[...]


def _mlp(x2d, g, w_up, w_down):
    m = x2d.shape[0]
    row = lambda i: (i, 0)
    const = lambda i: (0, 0)
    return pl.pallas_call(
        _mlp_kernel,
        grid=(m // ROW_TILE,),
        in_specs=[
            pl.BlockSpec((ROW_TILE, D_MODEL), row),
            pl.BlockSpec((1, D_MODEL), const),
            pl.BlockSpec((D_MODEL, D_FF), const),
            pl.BlockSpec((D_FF, D_MODEL), const),
        ],
        out_specs=pl.BlockSpec((ROW_TILE, D_MODEL), row),
        out_shape=jax.ShapeDtypeStruct((m, D_MODEL), f32),
        scratch_shapes=[pltpu.VMEM((ROW_TILE, D_MODEL), f32)],
        compiler_params=_cparams(1),
        name="mlp",
    )(x2d, g, w_up, w_down)


def _alibi_slopes():
    return jnp.exp2(-8.0 * jnp.arange(1, N_ALIBI + 1, dtype=f32) / N_ALIBI)


def _head_mean_matrix():
    r = jnp.arange(QK_CHUNK)[:, None] // HEAD_DIM
    c = jnp.arange(QK_CHUNK)[None, :] // HEAD_DIM
    return jnp.where(r == c, 1.0 / HEAD_DIM, 0.0).astype(bf16)


def _permute_families(a, axis):
    moba, fox, dil = jnp.split(a, [W_MOBA, W_MOBA + W_FOX], axis=axis)
    return jnp.concatenate([fox, dil, moba], axis=axis)


def kernel(x, attn_norm, w_in, b_forget, q_gain, k_gain, out_gain, w_out, mlp_norm, w_up, w_down):
    b, s, d = x.shape
    assert d == D_MODEL and s % ATT_TILE == 0 and (b * s) % ROW_TILE == 0
    depth = w_in.shape[0]
    slopes = _alibi_slopes()
    moba_lanes = jnp.concatenate(
        [_alibi_query_lanes(slopes[N_DIL:]), jnp.zeros((8 - N_MOBA, LANES), f32)])
    dil_lanes = _dilated_query_lanes(slopes[:N_DIL])
    gmat = _head_mean_matrix()
    x2d = x.reshape(b * s, d)
    for l in range(depth):
        wqkv, waux = _inproj_weights(w_in, l)
        baux = jnp.concatenate(
            [jnp.zeros((AUX_COLS - N_FOX,), f32), b_forget[l]]).reshape(1, AUX_COLS)
        gain = jnp.concatenate(
            [_permute_families(q_gain[l].reshape(1, d), 1) * (ATTN_SCALE * LOG2E),
             _permute_families(k_gain[l].reshape(1, d), 1)], axis=1)
        q, k, v, aux = _inproj(x2d, attn_norm[l].reshape(1, d), wqkv, waux, baux, gain, gmat)
        q3, k3, v3 = (a.reshape(b, s, d) for a in (q, k, v))
        cum = _fox_scan(aux.reshape(b, s, AUX_COLS))
        o_fox = _fox_attention(q3, k3, v3, cum)
        o_dil = _dilated_attention(dil_lanes, q3, k3, v3)
        o_moba = _moba_attention(moba_lanes, q3, k3, v3)
        x2d = _outproj(x2d, o_fox.reshape(b * s, W_FOX), o_dil.reshape(b * s, W_DIL),
                       o_moba.reshape(b * s, W_MOBA), aux,
                       _permute_families(out_gain[l].reshape(1, d), 1),
                       _permute_families(w_out[l], 0).astype(bf16))
        x2d = _mlp(x2d, mlp_norm[l].reshape(1, d), w_up[l].astype(bf16), w_down[l].astype(bf16))
    return x2d.reshape(b, s, d)
```

```python
import functools
import math

import jax
import jax.numpy as jnp
from jax import lax
from jax.experimental import pallas as pl
from jax.experimental.pallas import tpu as pltpu

D_MODEL = 1024
HEAD_DIM = 64
N_HEADS = D_MODEL // HEAD_DIM
N_MOBA = N_HEADS // 4
N_FOX = (N_HEADS - N_MOBA) // 2
N_DIL = N_HEADS - N_MOBA - N_FOX
N_ALIBI = N_MOBA + N_DIL
MOBA_BLOCK = 256
MOBA_TOPK = 3
DIL_PAIRS = ((128, 1), (512, 4), (2048, 16))
DIL_BLOCK = 128
D_FF = 4 * D_MODEL
QKV_COLS = 3 * D_MODEL
W_MOBA = N_MOBA * HEAD_DIM
W_FOX = N_FOX * HEAD_DIM
W_DIL = N_DIL * HEAD_DIM
ATTN_SCALE = HEAD_DIM ** -0.5
EPS = 1e-6
NEG_INF = -1e30
LOG2E = math.log2(math.e)

LANES = 128
AUX_COLS = 512
F_COL_BLOCK = AUX_COLS // LANES - 1
F_LANE0 = LANES - N_FOX
VMEM_LIMIT = 56 * 1024 * 1024

ROW_TILE = 512
QK_CHUNK = 512
ATT_TILE = 512
SCAN_TILE = 128
DIL_UNROLL = 8

OFF_FOX, OFF_DIL, OFF_MOBA = 0, W_FOX, W_FOX + W_DIL
MASK_VAL = -(2.0 ** 100)
N_SPLIT = 3

f32 = jnp.float32
bf16 = jnp.bfloat16


def _cparams(n_axes):
    return pltpu.CompilerParams(
        dimension_semantics=("arbitrary",) * n_axes,
        vmem_limit_bytes=VMEM_LIMIT,
    )


def _dot_nt(a, b):
    return lax.dot_general(a, b, (((1,), (1,)), ((), ())), preferred_element_type=f32)


def _lane_iota(shape):
    return lax.broadcasted_iota(jnp.int32, shape, len(shape) - 1)


def _split3(x):
    hi = x.astype(bf16).astype(f32)
    r = x - hi
    mid = r.astype(bf16).astype(f32)
    lo = (r - mid).astype(bf16).astype(f32)
    return hi, mid, lo


def _own_lanes(lane, h):
    return (lane < HEAD_DIM) if h == 0 else (lane >= HEAD_DIM)


def _spare_base(h):
    return HEAD_DIM if h == 0 else 0


def _cast_kernel(w_ref, o_ref):
    o_ref[...] = w_ref[0].astype(bf16)


def _cast_aux_kernel(w_ref, o_ref):
    col = _lane_iota((D_MODEL, AUX_COLS))
    w = jnp.where(col < N_FOX + W_FOX, w_ref[0], 0.0)
    o_ref[...] = pltpu.roll(w, AUX_COLS - N_FOX, 1).astype(bf16)


def _inproj_weights(w_in, l):
    d = D_MODEL
    n_lane_blocks = d // LANES
    moba_blocks = W_MOBA // LANES

    def src_block(j):
        return (j // n_lane_blocks) * n_lane_blocks + (j % n_lane_blocks + moba_blocks) % n_lane_blocks

    wqkv = pl.pallas_call(
        _cast_kernel,
        grid=(QKV_COLS // LANES,),
        in_specs=[pl.BlockSpec((1, d, LANES), lambda j: (l, 0, src_block(j)))],
        out_specs=pl.BlockSpec((d, LANES), lambda j: (0, j)),
        out_shape=jax.ShapeDtypeStruct((d, QKV_COLS), bf16),
        compiler_params=_cparams(1),
        name="cast_qkv",
    )(w_in)
    waux = pl.pallas_call(
        _cast_aux_kernel,
        grid=(1,),
        in_specs=[pl.BlockSpec((1, d, AUX_COLS), lambda j: (l, 0, QKV_COLS // AUX_COLS))],
        out_specs=pl.BlockSpec((d, AUX_COLS), lambda j: (0, 0)),
        out_shape=jax.ShapeDtypeStruct((d, AUX_COLS), bf16),
        compiler_params=_cparams(1),
        name="cast_aux",
    )(w_in)
    return wqkv, waux


DIL_GATHER = tuple(d for _, d in DIL_PAIRS if d > 1)


def _inproj_kernel(x_ref, g_ref, wqkv_ref, waux_ref, baux_ref, gain_ref, gmat_ref,
                   q_ref, k_ref, v_ref, aux_ref, *rest):
    res_refs, stage_ref = rest[:-1], rest[-1]
    x = x_ref[...]
    ms = jnp.mean(x * x, axis=-1, keepdims=True)
    hn = (x * lax.rsqrt(ms + EPS) * g_ref[...]).astype(bf16)
    n_chunks = QKV_COLS // QK_CHUNK
    per = D_MODEL // QK_CHUNK
    for c in range(n_chunks):
        cols = slice(c * QK_CHUNK, (c + 1) * QK_CHUNK)
        acc = jnp.dot(hn, wqkv_ref[:, cols], preferred_element_type=f32)
        tensor = c // per
        dst = (q_ref, k_ref, v_ref)[tensor]
        chunk0 = (c % per) * QK_CHUNK
        if tensor < 2:
            msq = jnp.dot((acc * acc).astype(bf16), gmat_ref[...], preferred_element_type=f32)
            acc = acc * lax.rsqrt(msq + EPS) * gain_ref[:, cols]
        dst[:, chunk0:chunk0 + QK_CHUNK] = acc.astype(bf16)
        for slab in range(W_DIL // LANES):
            col0 = OFF_DIL + slab * LANES - chunk0
            if 0 <= col0 < QK_CHUNK:
                stage_ref[tensor, slab] = acc[:, col0:col0 + LANES]
    aux_ref[...] = jnp.dot(hn, waux_ref[...], preferred_element_type=f32) + baux_ref[...]
    for tensor in range(3):
        for j, d in enumerate(DIL_GATHER):
            out = res_refs[tensor * len(DIL_GATHER) + j]
            for slab in range(W_DIL // LANES):
                for r in range(d):
                    rows = stage_ref[tensor, slab, pl.ds(r, ROW_TILE // d, stride=d), :]
                    out[0, r, :, slab * LANES:(slab + 1) * LANES] = rows.astype(bf16)


def _inproj(x2d, g, wqkv, waux, baux, gain, gmat, seq):
    m = x2d.shape[0]
    tiles_per_seq = seq // ROW_TILE
    row = lambda i: (i, 0)
    const = lambda i: (0, 0)
    res_specs, res_shapes = [], []
    for _ in range(3):
        for d in DIL_GATHER:
            res_specs.append(pl.BlockSpec((1, d, ROW_TILE // d, W_DIL),
                                          lambda i: (i // tiles_per_seq, 0, i % tiles_per_seq, 0)))
            res_shapes.append(jax.ShapeDtypeStruct((m // seq, d, seq // d, W_DIL), bf16))
    return pl.pallas_call(
        _inproj_kernel,
        grid=(m // ROW_TILE,),
        in_specs=[
            pl.BlockSpec((ROW_TILE, D_MODEL), row),
            pl.BlockSpec((1, D_MODEL), const),
            pl.BlockSpec((D_MODEL, QKV_COLS), const),
            pl.BlockSpec((D_MODEL, AUX_COLS), const),
            pl.BlockSpec((1, AUX_COLS), const),
            pl.BlockSpec((1, 2 * D_MODEL), const),
            pl.BlockSpec((QK_CHUNK, QK_CHUNK), const),
        ],
        out_specs=[
            pl.BlockSpec((ROW_TILE, D_MODEL), row),
            pl.BlockSpec((ROW_TILE, D_MODEL), row),
            pl.BlockSpec((ROW_TILE, D_MODEL), row),
            pl.BlockSpec((ROW_TILE, AUX_COLS), row),
        ] + res_specs,
        out_shape=[
            jax.ShapeDtypeStruct((m, D_MODEL), bf16),
            jax.ShapeDtypeStruct((m, D_MODEL), bf16),
            jax.ShapeDtypeStruct((m, D_MODEL), bf16),
            jax.ShapeDtypeStruct((m, AUX_COLS), f32),
        ] + res_shapes,
        scratch_shapes=[pltpu.VMEM((3, W_DIL // LANES, ROW_TILE, LANES), f32)],
        compiler_params=_cparams(1),
        name="inproj",
    )(x2d, g, wqkv, waux, baux, gain, gmat)


def _scan_kernel(f_ref, c_ref):
    n = f_ref.shape[1] // SCAN_TILE
    r = lax.broadcasted_iota(jnp.int32, (SCAN_TILE, SCAN_TILE), 0)
    c = lax.broadcasted_iota(jnp.int32, (SCAN_TILE, SCAN_TILE), 1)
    tri = jnp.where(r >= c, 1.0, 0.0).astype(f32)

    def body(i, carry):
        rows = pl.ds(pl.multiple_of(i * SCAN_TILE, SCAN_TILE), SCAN_TILE)
        f = f_ref[0, rows, :]
        ls = jnp.minimum(f, 0.0) - jnp.log(1.0 + jnp.exp(-jnp.abs(f)))
        cum = jnp.dot(tri, ls, preferred_element_type=f32,
                      precision=lax.Precision.HIGHEST) + carry
        c_ref[0, rows, :] = cum * LOG2E
        return cum[SCAN_TILE - 1:SCAN_TILE, :]

    lax.fori_loop(0, n, body, jnp.zeros((1, LANES), f32))


def _fox_scan(aux3):
    b, s, _ = aux3.shape
    return pl.pallas_call(
        _scan_kernel,
        grid=(b,),
        in_specs=[pl.BlockSpec((1, s, LANES), lambda i: (i, 0, F_COL_BLOCK))],
        out_specs=pl.BlockSpec((1, s, LANES), lambda i: (i, 0, 0)),
        out_shape=jax.ShapeDtypeStruct((b, s, LANES), f32),
        compiler_params=_cparams(1),
        name="fox_scan",
    )(aux3)


def _softmax_step(s, v_aug, m_ref, acc_ref, head):
    rows = s.shape[0]
    m_prev = m_ref[head]
    row_max = jnp.max(s, axis=-1, keepdims=True)
    m_new = jnp.maximum(m_prev, jnp.broadcast_to(row_max, (rows, LANES)))
    alpha = jnp.exp2(m_prev - m_new)
    m_wide = jnp.concatenate([m_new] * (s.shape[1] // LANES), axis=1)
    p = jnp.exp2(s - m_wide).astype(bf16)
    acc_ref[head] = alpha * acc_ref[head] + jnp.dot(p, v_aug, preferred_element_type=f32)
    m_ref[head] = m_new


def _ones_on_other_head(v2):
    first = _lane_iota(v2.shape) < HEAD_DIM
    one = jnp.ones_like(v2)
    return jnp.where(first, v2, one), jnp.where(first, one, v2)


def _finish_pair(acc_a, acc_b):
    o_a = acc_a / pltpu.roll(acc_a, HEAD_DIM, 1)
    o_b = acc_b / pltpu.roll(acc_b, HEAD_DIM, 1)
    return jnp.where(_lane_iota(acc_a.shape) < HEAD_DIM, o_a, o_b)


def _causal_chunk_then_past(i, n_heads, scores, values, m_ref, acc_ref, o_ref):
    t = ATT_TILE
    m_ref[...] = jnp.full(m_ref.shape, NEG_INF, f32)
    acc_ref[...] = jnp.zeros(acc_ref.shape, f32)

    def chunk(j, causal):
        rows = pl.ds(pl.multiple_of(j * t, t), t)
        for p in range(n_heads // 2):
            v_aug = _ones_on_other_head(values(rows, p))
            for h in range(2):
                head = 2 * p + h
                s = scores(head, rows)
                if causal:
                    r = lax.broadcasted_iota(jnp.int32, (t, t), 0)
                    c = lax.broadcasted_iota(jnp.int32, (t, t), 1)
                    s = jnp.where(c <= r, s, NEG_INF)
                _softmax_step(s, v_aug[h], m_ref, acc_ref, head)

    chunk(i, True)

    def body(j, carry):
        chunk(j, False)
        return carry

    lax.fori_loop(0, i, body, 0)
    for p in range(n_heads // 2):
        o_ref[0, :, p * LANES:(p + 1) * LANES] = _finish_pair(acc_ref[2 * p], acc_ref[2 * p + 1])


def _fox_kernel(q_ref, k_ref, v_ref, c_ref, o_ref, kaug_ref, m_ref, acc_ref):
    i = pl.program_id(1)
    t = ATT_TILE
    s_len = k_ref.shape[1]
    lane = _lane_iota((t, LANES))

    @pl.when(i == 0)
    def _():
        def build(r, carry):
            rows = pl.ds(pl.multiple_of(r * t, t), t)
            c = c_ref[0, rows, :]
            for p in range(N_FOX // 2):
                k2 = k_ref[0, rows, p * LANES:(p + 1) * LANES].astype(f32)
                for h in range(2):
                    head = 2 * p + h
                    base = _spare_base(h)
                    fl = F_LANE0 + head
                    pieces = _split3(-jnp.broadcast_to(c[:, fl:fl + 1], (t, LANES)))
                    ka = jnp.where(_own_lanes(lane, h), k2, 0.0)
                    for n, piece in enumerate(pieces):
                        ka = jnp.where(lane == base + n, piece, ka)
                    kaug_ref[head, rows, :] = ka.astype(bf16)
            return carry
        lax.fori_loop(0, s_len // t, build, 0)

    q_heads = []
    for p in range(N_FOX // 2):
        q2 = q_ref[0, :, p * LANES:(p + 1) * LANES].astype(f32)
        for h in range(2):
            base = _spare_base(h)
            ones = jnp.where((lane >= base) & (lane < base + N_SPLIT), 1.0, 0.0)
            q_heads.append(jnp.where(_own_lanes(lane, h), q2, ones).astype(bf16))

    _causal_chunk_then_past(
        i, N_FOX,
        lambda head, rows: _dot_nt(q_heads[head], kaug_ref[head, rows, :]),
        lambda rows, p: v_ref[0, rows, p * LANES:(p + 1) * LANES],
        m_ref, acc_ref, o_ref)


def _fox_attention(q, k, v, cum):
    b, s, _ = q.shape
    t = ATT_TILE
    blk = OFF_FOX // W_FOX
    return pl.pallas_call(
        _fox_kernel,
        grid=(b, s // t),
        in_specs=[
            pl.BlockSpec((1, t, W_FOX), lambda bi, i: (bi, i, blk)),
            pl.BlockSpec((1, s, W_FOX), lambda bi, i: (bi, 0, blk)),
            pl.BlockSpec((1, s, W_FOX), lambda bi, i: (bi, 0, blk)),
            pl.BlockSpec((1, s, LANES), lambda bi, i: (bi, 0, 0)),
        ],
        out_specs=pl.BlockSpec((1, t, W_FOX), lambda bi, i: (bi, i, 0)),
        out_shape=jax.ShapeDtypeStruct((b, s, W_FOX), f32),
        scratch_shapes=[
            pltpu.VMEM((N_FOX, s, LANES), bf16),
            pltpu.VMEM((N_FOX, t, LANES), f32),
            pltpu.VMEM((N_FOX, t, LANES), f32),
        ],
        compiler_params=_cparams(2),
        name="fox_attention",
    )(q, k, v, cum)


def _moba_kernel(qc_ref, q_ref, k_ref, v_ref, o_ref, kaug_ref, kmean_ref, m_ref, acc_ref):
    i = pl.program_id(1)
    t = ATT_TILE
    bs = MOBA_BLOCK
    s_len = k_ref.shape[1]
    n_blk = s_len // bs
    per_tile = t // bs
    lane = _lane_iota((t, LANES))
    mask_lane0 = 2 * N_SPLIT

    @pl.when(i == 0)
    def _():
        kmean_ref[...] = jnp.zeros(kmean_ref.shape, f32)

        def build(r, carry):
            rows = pl.ds(pl.multiple_of(r * t, t), t)
            pos = r * t + lax.broadcasted_iota(jnp.int32, (t, LANES), 0)
            blk = pos // bs
            in_blk = (pos - blk * bs).astype(f32)
            blk_start = (blk * bs).astype(f32)
            for p in range(N_MOBA // 2):
                k2 = k_ref[0, rows, p * LANES:(p + 1) * LANES].astype(f32)
                for u in range(per_tile):
                    kmean_ref[p, pl.ds(r * per_tile + u, 1), :] = jnp.mean(
                        k2[u * bs:(u + 1) * bs], axis=0, keepdims=True)
                for h in range(2):
                    base = _spare_base(h)
                    rel = lane - base
                    ka = jnp.where(_own_lanes(lane, h), k2, 0.0)
                    ka = jnp.where((rel >= 0) & (rel < N_SPLIT), in_blk, ka)
                    ka = jnp.where((rel >= N_SPLIT) & (rel < mask_lane0), blk_start, ka)
                    ka = jnp.where(rel - mask_lane0 == blk, MASK_VAL, ka)
                    kaug_ref[2 * p + h, rows, :] = ka.astype(bf16)
            return carry
        lax.fori_loop(0, s_len // t, build, 0)

    sub = lax.broadcasted_iota(jnp.int32, (n_blk, t), 0)
    q_blk = i * per_tile + lax.broadcasted_iota(jnp.int32, (1, t), 1) // bs
    q_heads = []
    for p in range(N_MOBA // 2):
        q2 = q_ref[0, :, p * LANES:(p + 1) * LANES].astype(f32)
        for h in range(2):
            head = 2 * p + h
            q_own = jnp.where(_own_lanes(lane, h), q2, 0.0)
            gate = lax.dot_general(kmean_ref[p, 0:n_blk, :], q_own, (((1,), (1,)), ((), ())),
                                   preferred_element_type=f32,
                                   precision=lax.Precision.HIGHEST)
            rank = jnp.zeros((n_blk, t), f32)
            for j in range(n_blk):
                gj = gate[j:j + 1, :]
                beats = (gj > gate) | ((gj == gate) & (j < sub))
                rank = rank + jnp.where(beats & (j < q_blk), 1.0, 0.0)
            dropped = jnp.where((sub < q_blk) & (rank >= MOBA_TOPK), 1.0, 0.0)
            dropped = jnp.concatenate([dropped, jnp.zeros((LANES - n_blk, t), f32)], axis=0)
            dropped = pltpu.roll(dropped.T, _spare_base(h) + mask_lane0, 1)
            q_heads.append((q_own + qc_ref[head:head + 1, :] + dropped).astype(bf16))

    _causal_chunk_then_past(
        i, N_MOBA,
        lambda head, rows: _dot_nt(q_heads[head], kaug_ref[head, rows, :]),
        lambda rows, p: v_ref[0, rows, p * LANES:(p + 1) * LANES],
        m_ref, acc_ref, o_ref)


def _moba_attention(q_const, q, k, v):
    b, s, _ = q.shape
    t = ATT_TILE
    blk = OFF_MOBA // W_MOBA
    assert 2 * N_SPLIT + s // MOBA_BLOCK <= HEAD_DIM and s // MOBA_BLOCK <= LANES
    return pl.pallas_call(
        _moba_kernel,
        grid=(b, s // t),
        in_specs=[
            pl.BlockSpec((8, LANES), lambda bi, i: (0, 0)),
            pl.BlockSpec((1, t, W_MOBA), lambda bi, i: (bi, i, blk)),
            pl.BlockSpec((1, s, W_MOBA), lambda bi, i: (bi, 0, blk)),
            pl.BlockSpec((1, s, W_MOBA), lambda bi, i: (bi, 0, blk)),
        ],
        out_specs=pl.BlockSpec((1, t, W_MOBA), lambda bi, i: (bi, i, 0)),
        out_shape=jax.ShapeDtypeStruct((b, s, W_MOBA), f32),
        scratch_shapes=[
            pltpu.VMEM((N_MOBA, s, LANES), bf16),
            pltpu.VMEM((N_MOBA // 2, LANES, LANES), f32),
            pltpu.VMEM((N_MOBA, t, LANES), f32),
            pltpu.VMEM((N_MOBA, t, LANES), f32),
        ],
        compiler_params=_cparams(2),
        name="moba_attention",
    )(q_const, q, k, v)


def _alibi_query_lanes(slopes):
    n = slopes.shape[0]
    pieces = _split3(slopes * LOG2E)
    lanes = jnp.arange(LANES)
    rows = []
    for head in range(n):
        rel = lanes - _spare_base(head % 2)
        row = jnp.zeros((LANES,), f32)
        for j in range(2 * N_SPLIT):
            row = jnp.where(rel == j, pieces[j % N_SPLIT][head], row)
        rows.append(row)
    return jnp.stack(rows)


def _dilated_kernel(qc_ref, *refs):
    n_br = len(DIL_PAIRS)
    qkv_refs = [refs[3 * g:3 * g + 3] for g in range(n_br)]
    o_ref, kpos_ref, m_ref, acc_ref = refs[3 * n_br:]
    s_len = o_ref.shape[1]
    blk = DIL_BLOCK
    n_blk = s_len // blk
    lane = _lane_iota((blk, LANES))
    lane_k = _lane_iota((2 * blk, LANES))

    @pl.when((pl.program_id(0) == 0) & (pl.program_id(1) == 0))
    def _():
        ch = 2 * blk
        lane_c = _lane_iota((ch, LANES))
        for g, (_, d) in enumerate(DIL_PAIRS):
            per_res = s_len // d // ch

            def fill(u, carry, g=g, d=d, per_res=per_res):
                res = u // per_res
                sub0 = (u - res * per_res) * ch
                pos = res + d * (sub0 + lax.broadcasted_iota(jnp.int32, (ch, LANES), 0))
                pos_lo = jnp.bitwise_and(pos, blk - 1)
                pos_hi = (pos - pos_lo).astype(f32)
                pos_lo = pos_lo.astype(f32)
                for h in range(2):
                    rel = lane_c - _spare_base(h)
                    val = jnp.where((rel >= 0) & (rel < N_SPLIT), pos_lo,
                                    jnp.where((rel >= N_SPLIT) & (rel < 2 * N_SPLIT), pos_hi, 0.0))
                    kpos_ref[2 * g + h, pl.ds(pl.multiple_of(u * ch, ch), ch), :] = val.astype(bf16)
                return carry

            lax.fori_loop(0, s_len // ch, fill, 0)

    q_const = [jnp.broadcast_to(qc_ref[0, h:h + 1, :], (blk, LANES)).astype(bf16) for h in range(2)]
    iq = lax.broadcasted_iota(jnp.int32, (blk, 2 * blk), 0) + blk
    ik = lax.broadcasted_iota(jnp.int32, (blk, 2 * blk), 1)
    off = iq - ik
    order = sorted(range(n_br), key=lambda g: -DIL_PAIRS[g][1])
    for step, g in enumerate(order):
        window, d = DIL_PAIRS[g]
        q_ref, k_ref, v_ref = qkv_refs[g]
        per_res = n_blk // d
        in_band = (off >= 0) & (off <= window // d)
        first, last = step == 0, step == n_br - 1
        assert not last or d == 1

        def blocks(it, carry, g=g, d=d, per_res=per_res, in_band=in_band, first=first, last=last,
                   q_ref=q_ref, k_ref=k_ref, v_ref=v_ref):
            for u in range(DIL_UNROLL):
                nb = it * DIL_UNROLL + u
                res = nb // per_res
                n = nb - res * per_res
                own = pl.ds(pl.multiple_of(n * blk, blk), blk)
                prev = pl.ds(pl.multiple_of(jnp.maximum(n - 1, 0) * blk, blk), blk)
                flat_own = pl.ds(pl.multiple_of(nb * blk, blk), blk)
                flat_prev = pl.ds(pl.multiple_of(jnp.maximum(nb - 1, 0) * blk, blk), blk)
                q2 = q_ref[0, res, own, :]
                k2 = jnp.concatenate([k_ref[0, res, prev, :], k_ref[0, res, own, :]], axis=0)
                v2 = jnp.concatenate([v_ref[0, res, prev, :], v_ref[0, res, own, :]], axis=0)
                v_aug = _ones_on_other_head(v2)
                first_key = jnp.where(n > 0, 0, blk)
                allowed = in_band & (ik >= first_key)
                if d > 1:
                    nat = pl.ds(res + n * (blk * d), blk, stride=d)
                else:
                    nat = flat_own
                outs = []
                for h in range(2):
                    kp = jnp.concatenate([kpos_ref[2 * g + h, flat_prev, :],
                                          kpos_ref[2 * g + h, flat_own, :]], axis=0)
                    ka = jnp.where(_own_lanes(lane_k, h), k2, kp)
                    qa = jnp.where(_own_lanes(lane, h), q2, q_const[h])
                    s = jnp.where(allowed, _dot_nt(qa, ka), NEG_INF)
                    m_new = jnp.broadcast_to(jnp.max(s, axis=-1, keepdims=True), (blk, LANES))
                    if not first:
                        m_prev = m_ref[h, nat, :]
                        m_new = jnp.maximum(m_prev, m_new)
                    pr = jnp.exp2(s - jnp.concatenate([m_new, m_new], axis=1)).astype(bf16)
                    acc = jnp.dot(pr, v_aug[h], preferred_element_type=f32)
                    if not first:
                        acc = jnp.exp2(m_prev - m_new) * acc_ref[h, nat, :] + acc
                    if last:
                        outs.append(acc)
                    else:
                        m_ref[h, nat, :] = m_new
                        acc_ref[h, nat, :] = acc
                if last:
                    o_ref[0, nat, :] = _finish_pair(outs[0], outs[1])
            return carry

        lax.fori_loop(0, n_blk // DIL_UNROLL, blocks, 0)


def _dilated_attention(q_const, branch_qkv):
    b, _, s, _ = branch_qkv[0][0][0].shape
    n_br = len(DIL_PAIRS)
    assert s % (2 * DIL_BLOCK * DIL_PAIRS[-1][1]) == 0 and (s // DIL_BLOCK) % DIL_UNROLL == 0
    in_specs = [pl.BlockSpec((1, 8, LANES), lambda bi, p: (p, 0, 0))]
    args = [q_const]
    for (qkv, base), (_, d) in zip(branch_qkv, DIL_PAIRS):
        spec = pl.BlockSpec((1, d, s // d, LANES), lambda bi, p, base=base: (bi, 0, 0, base + p))
        in_specs += [spec] * 3
        args += list(qkv)
    return pl.pallas_call(
        _dilated_kernel,
        grid=(b, W_DIL // LANES),
        in_specs=in_specs,
        out_specs=pl.BlockSpec((1, s, LANES), lambda bi, p: (bi, 0, p)),
        out_shape=jax.ShapeDtypeStruct((b, s, W_DIL), f32),
        scratch_shapes=[
            pltpu.VMEM((2 * n_br, s, LANES), bf16),
            pltpu.VMEM((2, s, LANES), f32),
            pltpu.VMEM((2, s, LANES), f32),
        ],
        compiler_params=_cparams(2),
        name="dilated_attention",
    )(*args)


def _dilated_query_lanes(slopes):
    rows = _alibi_query_lanes(slopes)
    pairs = rows.reshape(N_DIL // 2, 2, LANES)
    return jnp.concatenate([pairs, jnp.zeros((N_DIL // 2, 6, LANES), f32)], axis=1)


def _rms(o, gain):
    return o * lax.rsqrt(jnp.mean(o * o, axis=-1, keepdims=True) + EPS) * gain


def _outproj_kernel(x_ref, of_ref, od_ref, om_ref, gf_ref, gain_ref, w_ref, out_ref, y_ref):
    a0, a1 = OFF_DIL, OFF_MOBA
    y_fox = _rms(of_ref[...], gain_ref[:, 0:a0]) * jax.nn.sigmoid(gf_ref[...])
    y_ref[:, 0:a0] = y_fox.astype(bf16)
    y_ref[:, a0:a1] = _rms(od_ref[...], gain_ref[:, a0:a1]).astype(bf16)
    y_ref[:, a1:] = _rms(om_ref[...], gain_ref[:, a1:]).astype(bf16)
    out_ref[...] = x_ref[...] + jnp.dot(y_ref[...], w_ref[...], preferred_element_type=f32)


def _outproj(x2d, o_fox, o_dil, o_moba, aux, gain, w_out):
    m = x2d.shape[0]
    row = lambda i: (i, 0)
    const = lambda i: (0, 0)
    return pl.pallas_call(
        _outproj_kernel,
        grid=(m // ROW_TILE,),
        in_specs=[
            pl.BlockSpec((ROW_TILE, D_MODEL), row),
            pl.BlockSpec((ROW_TILE, W_FOX), row),
            pl.BlockSpec((ROW_TILE, W_DIL), row),
            pl.BlockSpec((ROW_TILE, W_MOBA), row),
            pl.BlockSpec((ROW_TILE, W_FOX), row),
            pl.BlockSpec((1, D_MODEL), const),
            pl.BlockSpec((D_MODEL, D_MODEL), const),
        ],
        out_specs=pl.BlockSpec((ROW_TILE, D_MODEL), row),
        out_shape=jax.ShapeDtypeStruct((m, D_MODEL), f32),
        scratch_shapes=[pltpu.VMEM((ROW_TILE, D_MODEL), bf16)],
        compiler_params=_cparams(1),
        name="outproj",
    )(x2d, o_fox, o_dil, o_moba, aux, gain, w_out)


def _mlp_kernel(x_ref, g_ref, wup_ref, wdown_ref, out_ref, acc_ref):
    x = x_ref[...]
    hn = _rms(x, g_ref[...]).astype(bf16)
    acc_ref[...] = x
    for c in range(D_FF // QK_CHUNK):
        cols = slice(c * QK_CHUNK, (c + 1) * QK_CHUNK)
        h = jnp.maximum(jnp.dot(hn, wup_ref[:, cols], preferred_element_type=f32), 0.0)
        acc_ref[...] += jnp.dot((h * h).astype(bf16), wdown_ref[cols, :],
                                preferred_element_type=f32)
    out_ref[...] = acc_ref[...]


def _mlp(x2d, g, w_up, w_down):
    m = x2d.shape[0]
    row = lambda i: (i, 0)
    const = lambda i: (0, 0)
    return pl.pallas_call(
        _mlp_kernel,
        grid=(m // ROW_TILE,),
        in_specs=[
            pl.BlockSpec((ROW_TILE, D_MODEL), row),
            pl.BlockSpec((1, D_MODEL), const),
            pl.BlockSpec((D_MODEL, D_FF), const),
            pl.BlockSpec((D_FF, D_MODEL), const),
        ],
        out_specs=pl.BlockSpec((ROW_TILE, D_MODEL), row),
        out_shape=jax.ShapeDtypeStruct((m, D_MODEL), f32),
        scratch_shapes=[pltpu.VMEM((ROW_TILE, D_MODEL), f32)],
        compiler_params=_cparams(1),
        name="mlp",
    )(x2d, g, w_up, w_down)


def _alibi_slopes():
    return jnp.exp2(-8.0 * jnp.arange(1, N_ALIBI + 1, dtype=f32) / N_ALIBI)


def _head_mean_matrix():
    r = jnp.arange(QK_CHUNK)[:, None] // HEAD_DIM
    c = jnp.arange(QK_CHUNK)[None, :] // HEAD_DIM
    return jnp.where(r == c, 1.0 / HEAD_DIM, 0.0).astype(bf16)


def _permute_families(a, axis):
    moba, fox, dil = jnp.split(a, [W_MOBA, W_MOBA + W_FOX], axis=axis)
    return jnp.concatenate([fox, dil, moba], axis=axis)


def kernel(x, attn_norm, w_in, b_forget, q_gain, k_gain, out_gain, w_out, mlp_norm, w_up, w_down):
    b, s, d = x.shape
    assert d == D_MODEL and s % ATT_TILE == 0 and (b * s) % ROW_TILE == 0
    depth = w_in.shape[0]
    slopes = _alibi_slopes()
    moba_lanes = jnp.concatenate(
        [_alibi_query_lanes(slopes[N_DIL:]), jnp.zeros((8 - N_MOBA, LANES), f32)])
    dil_lanes = _dilated_query_lanes(slopes[:N_DIL])
    gmat = _head_mean_matrix()
    x2d = x.reshape(b * s, d)
    for l in range(depth):
        wqkv, waux = _inproj_weights(w_in, l)
        baux = jnp.concatenate(
            [jnp.zeros((AUX_COLS - N_FOX,), f32), b_forget[l]]).reshape(1, AUX_COLS)
        gain = jnp.concatenate(
            [_permute_families(q_gain[l].reshape(1, d), 1) * (ATTN_SCALE * LOG2E),
             _permute_families(k_gain[l].reshape(1, d), 1)], axis=1)
        q, k, v, aux, *gathered = _inproj(x2d, attn_norm[l].reshape(1, d), wqkv, waux, baux, gain,
                                          gmat, s)
        q3, k3, v3 = (a.reshape(b, s, d) for a in (q, k, v))
        branches = [((q3.reshape(b, 1, s, d), k3.reshape(b, 1, s, d), v3.reshape(b, 1, s, d)),
                     OFF_DIL // LANES)]
        for j in range(len(DIL_GATHER)):
            branches.append((tuple(gathered[t * len(DIL_GATHER) + j] for t in range(3)), 0))
        cum = _fox_scan(aux.reshape(b, s, AUX_COLS))
        o_fox = _fox_attention(q3, k3, v3, cum)
        o_dil = _dilated_attention(dil_lanes, branches)
        o_moba = _moba_attention(moba_lanes, q3, k3, v3)
        x2d = _outproj(x2d, o_fox.reshape(b * s, W_FOX), o_dil.reshape(b * s, W_DIL),
                       o_moba.reshape(b * s, W_MOBA), aux,
                       _permute_families(out_gain[l].reshape(1, d), 1),
                       _permute_families(w_out[l], 0).astype(bf16))
        x2d = _mlp(x2d, mlp_norm[l].reshape(1, d), w_up[l].astype(bf16), w_down[l].astype(bf16))
    return x2d.reshape(b, s, d)
```

```python
import functools
import math

import jax
import jax.numpy as jnp
from jax import lax
from jax.experimental import pallas as pl
from jax.experimental.pallas import tpu as pltpu

D_MODEL = 1024
HEAD_DIM = 64
N_HEADS = D_MODEL // HEAD_DIM
N_MOBA = N_HEADS // 4
N_FOX = (N_HEADS - N_MOBA) // 2
N_DIL = N_HEADS - N_MOBA - N_FOX
N_ALIBI = N_MOBA + N_DIL
MOBA_BLOCK = 256
MOBA_TOPK = 3
DIL_PAIRS = ((128, 1), (512, 4), (2048, 16))
DIL_BLOCK = 128
D_FF = 4 * D_MODEL
QKV_COLS = 3 * D_MODEL
W_MOBA = N_MOBA * HEAD_DIM
W_FOX = N_FOX * HEAD_DIM
W_DIL = N_DIL * HEAD_DIM
ATTN_SCALE = HEAD_DIM ** -0.5
EPS = 1e-6
NEG_INF = -1e30
LOG2E = math.log2(math.e)

LANES = 128
AUX_COLS = 512
F_COL_BLOCK = AUX_COLS // LANES - 1
F_LANE0 = LANES - N_FOX
VMEM_LIMIT = 56 * 1024 * 1024

ROW_TILE = 512
QK_CHUNK = 512
ATT_TILE = 512
SCAN_TILE = 128
DIL_UNROLL = 8

OFF_FOX, OFF_DIL, OFF_MOBA = 0, W_FOX, W_FOX + W_DIL
MASK_VAL = -(2.0 ** 100)
N_SPLIT = 3

f32 = jnp.float32
bf16 = jnp.bfloat16


def _cparams(n_axes):
    return pltpu.CompilerParams(
        dimension_semantics=("arbitrary",) * n_axes,
        vmem_limit_bytes=VMEM_LIMIT,
    )


def _dot_nt(a, b):
    return lax.dot_general(a, b, (((1,), (1,)), ((), ())), preferred_element_type=f32)


def _lane_iota(shape):
    return lax.broadcasted_iota(jnp.int32, shape, len(shape) - 1)


def _split3(x):
    hi = x.astype(bf16).astype(f32)
    r = x - hi
    mid = r.astype(bf16).astype(f32)
    lo = (r - mid).astype(bf16).astype(f32)
    return hi, mid, lo


def _own_lanes(lane, h):
    return (lane < HEAD_DIM) if h == 0 else (lane >= HEAD_DIM)


def _spare_base(h):
    return HEAD_DIM if h == 0 else 0


def _cast_kernel(w_ref, o_ref):
    o_ref[...] = w_ref[0].astype(bf16)


def _cast_aux_kernel(w_ref, o_ref):
    col = _lane_iota((D_MODEL, AUX_COLS))
    w = jnp.where(col < N_FOX + W_FOX, w_ref[0], 0.0)
    o_ref[...] = pltpu.roll(w, AUX_COLS - N_FOX, 1).astype(bf16)


def _inproj_weights(w_in, l):
    d = D_MODEL
    n_lane_blocks = d // LANES
    moba_blocks = W_MOBA // LANES

    def src_block(j):
        return (j // n_lane_blocks) * n_lane_blocks + (j % n_lane_blocks + moba_blocks) % n_lane_blocks

    wqkv = pl.pallas_call(
        _cast_kernel,
        grid=(QKV_COLS // LANES,),
        in_specs=[pl.BlockSpec((1, d, LANES), lambda j: (l, 0, src_block(j)))],
        out_specs=pl.BlockSpec((d, LANES), lambda j: (0, j)),
        out_shape=jax.ShapeDtypeStruct((d, QKV_COLS), bf16),
        compiler_params=_cparams(1),
        name="cast_qkv",
    )(w_in)
    waux = pl.pallas_call(
        _cast_aux_kernel,
        grid=(1,),
        in_specs=[pl.BlockSpec((1, d, AUX_COLS), lambda j: (l, 0, QKV_COLS // AUX_COLS))],
        out_specs=pl.BlockSpec((d, AUX_COLS), lambda j: (0, 0)),
        out_shape=jax.ShapeDtypeStruct((d, AUX_COLS), bf16),
        compiler_params=_cparams(1),
        name="cast_aux",
    )(w_in)
    return wqkv, waux


DIL_GATHER = tuple(d for _, d in DIL_PAIRS if d > 1)


def _inproj_kernel(x_ref, g_ref, wqkv_ref, waux_ref, baux_ref, gain_ref, gmat_ref,
                   q_ref, k_ref, v_ref, aux_ref, *rest):
    res_refs, stage_ref = rest[:-1], rest[-1]
    x = x_ref[...]
    ms = jnp.mean(x * x, axis=-1, keepdims=True)
    hn = (x * lax.rsqrt(ms + EPS) * g_ref[...]).astype(bf16)
    n_chunks = QKV_COLS // QK_CHUNK
    per = D_MODEL // QK_CHUNK
    for c in range(n_chunks):
        cols = slice(c * QK_CHUNK, (c + 1) * QK_CHUNK)
        acc = jnp.dot(hn, wqkv_ref[:, cols], preferred_element_type=f32)
        tensor = c // per
        dst = (q_ref, k_ref, v_ref)[tensor]
        chunk0 = (c % per) * QK_CHUNK
        if tensor < 2:
            msq = jnp.dot((acc * acc).astype(bf16), gmat_ref[...], preferred_element_type=f32)
            acc = acc * lax.rsqrt(msq + EPS) * gain_ref[:, cols]
        dst[:, chunk0:chunk0 + QK_CHUNK] = acc.astype(bf16)
        for slab in range(W_DIL // LANES):
            col0 = OFF_DIL + slab * LANES - chunk0
            if 0 <= col0 < QK_CHUNK:
                stage_ref[tensor, slab] = acc[:, col0:col0 + LANES]
    aux_ref[...] = jnp.dot(hn, waux_ref[...], preferred_element_type=f32) + baux_ref[...]
    for tensor in range(3):
        for j, d in enumerate(DIL_GATHER):
            out = res_refs[tensor * len(DIL_GATHER) + j]
            for slab in range(W_DIL // LANES):
                for r in range(d):
                    rows = stage_ref[tensor, slab, pl.ds(r, ROW_TILE // d, stride=d), :]
                    out[0, r, :, slab * LANES:(slab + 1) * LANES] = rows.astype(bf16)


def _inproj(x2d, g, wqkv, waux, baux, gain, gmat, seq):
    m = x2d.shape[0]
    tiles_per_seq = seq // ROW_TILE
    row = lambda i: (i, 0)
    const = lambda i: (0, 0)
    res_specs, res_shapes = [], []
    for _ in range(3):
        for d in DIL_GATHER:
            res_specs.append(pl.BlockSpec((1, d, ROW_TILE // d, W_DIL),
                                          lambda i: (i // tiles_per_seq, 0, i % tiles_per_seq, 0)))
            res_shapes.append(jax.ShapeDtypeStruct((m // seq, d, seq // d, W_DIL), bf16))
    return pl.pallas_call(
        _inproj_kernel,
        grid=(m // ROW_TILE,),
        in_specs=[
            pl.BlockSpec((ROW_TILE, D_MODEL), row),
            pl.BlockSpec((1, D_MODEL), const),
            pl.BlockSpec((D_MODEL, QKV_COLS), const),
            pl.BlockSpec((D_MODEL, AUX_COLS), const),
            pl.BlockSpec((1, AUX_COLS), const),
            pl.BlockSpec((1, 2 * D_MODEL), const),
            pl.BlockSpec((QK_CHUNK, QK_CHUNK), const),
        ],
        out_specs=[
            pl.BlockSpec((ROW_TILE, D_MODEL), row),
            pl.BlockSpec((ROW_TILE, D_MODEL), row),
            pl.BlockSpec((ROW_TILE, D_MODEL), row),
            pl.BlockSpec((ROW_TILE, AUX_COLS), row),
        ] + res_specs,
        out_shape=[
            jax.ShapeDtypeStruct((m, D_MODEL), bf16),
            jax.ShapeDtypeStruct((m, D_MODEL), bf16),
            jax.ShapeDtypeStruct((m, D_MODEL), bf16),
            jax.ShapeDtypeStruct((m, AUX_COLS), f32),
        ] + res_shapes,
        scratch_shapes=[pltpu.VMEM((3, W_DIL // LANES, ROW_TILE, LANES), f32)],
        compiler_params=_cparams(1),
        name="inproj",
    )(x2d, g, wqkv, waux, baux, gain, gmat)


def _scan_kernel(f_ref, c_ref):
    n = f_ref.shape[1] // SCAN_TILE
    r = lax.broadcasted_iota(jnp.int32, (SCAN_TILE, SCAN_TILE), 0)
    c = lax.broadcasted_iota(jnp.int32, (SCAN_TILE, SCAN_TILE), 1)
    tri = jnp.where(r >= c, 1.0, 0.0).astype(f32)

    def body(i, carry):
        rows = pl.ds(pl.multiple_of(i * SCAN_TILE, SCAN_TILE), SCAN_TILE)
        f = f_ref[0, rows, :]
        ls = jnp.minimum(f, 0.0) - jnp.log(1.0 + jnp.exp(-jnp.abs(f)))
        cum = jnp.dot(tri, ls, preferred_element_type=f32,
                      precision=lax.Precision.HIGHEST) + carry
        c_ref[0, rows, :] = cum * LOG2E
        return cum[SCAN_TILE - 1:SCAN_TILE, :]

    lax.fori_loop(0, n, body, jnp.zeros((1, LANES), f32))


def _fox_scan(aux3):
    b, s, _ = aux3.shape
    return pl.pallas_call(
        _scan_kernel,
        grid=(b,),
        in_specs=[pl.BlockSpec((1, s, LANES), lambda i: (i, 0, F_COL_BLOCK))],
        out_specs=pl.BlockSpec((1, s, LANES), lambda i: (i, 0, 0)),
        out_shape=jax.ShapeDtypeStruct((b, s, LANES), f32),
        compiler_params=_cparams(1),
        name="fox_scan",
    )(aux3)


def _softmax_step(s, v_aug, m_ref, acc_ref, head, rows=slice(None)):
    m_prev = m_ref[head, rows, :]
    row_max = jnp.max(s, axis=-1, keepdims=True)
    m_new = jnp.maximum(m_prev, jnp.broadcast_to(row_max, m_prev.shape))
    alpha = jnp.exp2(m_prev - m_new)
    m_wide = jnp.concatenate([m_new] * (s.shape[1] // LANES), axis=1)
    p = jnp.exp2(s - m_wide).astype(bf16)
    acc_ref[head, rows, :] = (alpha * acc_ref[head, rows, :]
                              + jnp.dot(p, v_aug, preferred_element_type=f32))
    m_ref[head, rows, :] = m_new


def _ones_on_other_head(v2):
    first = _lane_iota(v2.shape) < HEAD_DIM
    one = jnp.ones_like(v2)
    return jnp.where(first, v2, one), jnp.where(first, one, v2)


def _finish_pair(acc_a, acc_b):
    o_a = acc_a / pltpu.roll(acc_a, HEAD_DIM, 1)
    o_b = acc_b / pltpu.roll(acc_b, HEAD_DIM, 1)
    return jnp.where(_lane_iota(acc_a.shape) < HEAD_DIM, o_a, o_b)


def _causal_chunk_then_past(i, n_heads, scores, values, m_ref, acc_ref, o_ref):
    t = ATT_TILE
    m_ref[...] = jnp.full(m_ref.shape, NEG_INF, f32)
    acc_ref[...] = jnp.zeros(acc_ref.shape, f32)

    def chunk(j, causal):
        rows = pl.ds(pl.multiple_of(j * t, t), t)
        for p in range(n_heads // 2):
            v_aug = _ones_on_other_head(values(rows, p))
            for h in range(2):
                head = 2 * p + h
                s = scores(head, slice(None), rows)
                if causal:
                    r = lax.broadcasted_iota(jnp.int32, (t, t), 0)
                    c = lax.broadcasted_iota(jnp.int32, (t, t), 1)
                    s = jnp.where(c <= r, s, NEG_INF)
                _softmax_step(s, v_aug[h], m_ref, acc_ref, head)

    chunk(i, True)

    def body(j, carry):
        chunk(j, False)
        return carry

    lax.fori_loop(0, i, body, 0)
    for p in range(n_heads // 2):
        o_ref[0, :, p * LANES:(p + 1) * LANES] = _finish_pair(acc_ref[2 * p], acc_ref[2 * p + 1])


def _fox_kernel(q_ref, k_ref, v_ref, c_ref, o_ref, kaug_ref, m_ref, acc_ref):
    i = pl.program_id(1)
    t = ATT_TILE
    s_len = k_ref.shape[1]
    lane = _lane_iota((t, LANES))

    @pl.when(i == 0)
    def _():
        def build(r, carry):
            rows = pl.ds(pl.multiple_of(r * t, t), t)
            c = c_ref[0, rows, :]
            for p in range(N_FOX // 2):
                k2 = k_ref[0, rows, p * LANES:(p + 1) * LANES].astype(f32)
                for h in range(2):
                    head = 2 * p + h
                    base = _spare_base(h)
                    fl = F_LANE0 + head
                    pieces = _split3(-jnp.broadcast_to(c[:, fl:fl + 1], (t, LANES)))
                    ka = jnp.where(_own_lanes(lane, h), k2, 0.0)
                    for n, piece in enumerate(pieces):
                        ka = jnp.where(lane == base + n, piece, ka)
                    kaug_ref[head, rows, :] = ka.astype(bf16)
            return carry
        lax.fori_loop(0, s_len // t, build, 0)

    q_heads = []
    for p in range(N_FOX // 2):
        q2 = q_ref[0, :, p * LANES:(p + 1) * LANES].astype(f32)
        for h in range(2):
            base = _spare_base(h)
            ones = jnp.where((lane >= base) & (lane < base + N_SPLIT), 1.0, 0.0)
            q_heads.append(jnp.where(_own_lanes(lane, h), q2, ones).astype(bf16))

    _causal_chunk_then_past(
        i, N_FOX,
        lambda head, q_rows, k_rows: _dot_nt(q_heads[head][q_rows], kaug_ref[head, k_rows, :]),
        lambda rows, p: v_ref[0, rows, p * LANES:(p + 1) * LANES],
        m_ref, acc_ref, o_ref)


def _fox_attention(q, k, v, cum):
    b, s, _ = q.shape
    t = ATT_TILE
    blk = OFF_FOX // W_FOX
    return pl.pallas_call(
        _fox_kernel,
        grid=(b, s // t),
        in_specs=[
            pl.BlockSpec((1, t, W_FOX), lambda bi, i: (bi, i, blk)),
            pl.BlockSpec((1, s, W_FOX), lambda bi, i: (bi, 0, blk)),
            pl.BlockSpec((1, s, W_FOX), lambda bi, i: (bi, 0, blk)),
            pl.BlockSpec((1, s, LANES), lambda bi, i: (bi, 0, 0)),
        ],
        out_specs=pl.BlockSpec((1, t, W_FOX), lambda bi, i: (bi, i, 0)),
        out_shape=jax.ShapeDtypeStruct((b, s, W_FOX), f32),
        scratch_shapes=[
            pltpu.VMEM((N_FOX, s, LANES), bf16),
            pltpu.VMEM((N_FOX, t, LANES), f32),
            pltpu.VMEM((N_FOX, t, LANES), f32),
        ],
        compiler_params=_cparams(2),
        name="fox_attention",
    )(q, k, v, cum)


def _moba_kernel(qc_ref, q_ref, k_ref, v_ref, o_ref, kaug_ref, kmean_ref, m_ref, acc_ref):
    i = pl.program_id(1)
    t = ATT_TILE
    bs = MOBA_BLOCK
    s_len = k_ref.shape[1]
    n_blk = s_len // bs
    per_tile = t // bs
    lane = _lane_iota((t, LANES))
    mask_lane0 = 2 * N_SPLIT

    @pl.when(i == 0)
    def _():
        kmean_ref[...] = jnp.zeros(kmean_ref.shape, f32)

        def build(r, carry):
            rows = pl.ds(pl.multiple_of(r * t, t), t)
            pos = r * t + lax.broadcasted_iota(jnp.int32, (t, LANES), 0)
            blk = pos // bs
            in_blk = (pos - blk * bs).astype(f32)
            blk_start = (blk * bs).astype(f32)
            for p in range(N_MOBA // 2):
                k2 = k_ref[0, rows, p * LANES:(p + 1) * LANES].astype(f32)
                for u in range(per_tile):
                    kmean_ref[p, pl.ds(r * per_tile + u, 1), :] = jnp.mean(
                        k2[u * bs:(u + 1) * bs], axis=0, keepdims=True)
                for h in range(2):
                    base = _spare_base(h)
                    rel = lane - base
                    ka = jnp.where(_own_lanes(lane, h), k2, 0.0)
                    ka = jnp.where((rel >= 0) & (rel < N_SPLIT), in_blk, ka)
                    ka = jnp.where((rel >= N_SPLIT) & (rel < mask_lane0), blk_start, ka)
                    ka = jnp.where(rel - mask_lane0 == blk, MASK_VAL, ka)
                    kaug_ref[2 * p + h, rows, :] = ka.astype(bf16)
            return carry
        lax.fori_loop(0, s_len // t, build, 0)

    sub = lax.broadcasted_iota(jnp.int32, (n_blk, t), 0)
    q_blk = i * per_tile + lax.broadcasted_iota(jnp.int32, (1, t), 1) // bs
    q_heads = []
    for p in range(N_MOBA // 2):
        q2 = q_ref[0, :, p * LANES:(p + 1) * LANES].astype(f32)
        for h in range(2):
            head = 2 * p + h
            q_own = jnp.where(_own_lanes(lane, h), q2, 0.0)
            gate = lax.dot_general(kmean_ref[p, 0:n_blk, :], q_own, (((1,), (1,)), ((), ())),
                                   preferred_element_type=f32,
                                   precision=lax.Precision.HIGHEST)
            rank = jnp.zeros((n_blk, t), f32)
            for j in range(n_blk):
                gj = gate[j:j + 1, :]
                beats = (gj > gate) | ((gj == gate) & (j < sub))
                rank = rank + jnp.where(beats & (j < q_blk), 1.0, 0.0)
            dropped = jnp.where((sub < q_blk) & (rank >= MOBA_TOPK), 1.0, 0.0)
            dropped = jnp.concatenate([dropped, jnp.zeros((LANES - n_blk, t), f32)], axis=0)
            dropped = pltpu.roll(dropped.T, _spare_base(h) + mask_lane0, 1)
            q_heads.append((q_own + qc_ref[head:head + 1, :] + dropped).astype(bf16))

    _causal_chunk_then_past(
        i, N_MOBA,
        lambda head, q_rows, k_rows: _dot_nt(q_heads[head][q_rows], kaug_ref[head, k_rows, :]),
        lambda rows, p: v_ref[0, rows, p * LANES:(p + 1) * LANES],
        m_ref, acc_ref, o_ref)


def _moba_attention(q_const, q, k, v):
    b, s, _ = q.shape
    t = ATT_TILE
    blk = OFF_MOBA // W_MOBA
    assert 2 * N_SPLIT + s // MOBA_BLOCK <= HEAD_DIM and s // MOBA_BLOCK <= LANES
    return pl.pallas_call(
        _moba_kernel,
        grid=(b, s // t),
        in_specs=[
            pl.BlockSpec((8, LANES), lambda bi, i: (0, 0)),
            pl.BlockSpec((1, t, W_MOBA), lambda bi, i: (bi, i, blk)),
            pl.BlockSpec((1, s, W_MOBA), lambda bi, i: (bi, 0, blk)),
            pl.BlockSpec((1, s, W_MOBA), lambda bi, i: (bi, 0, blk)),
        ],
        out_specs=pl.BlockSpec((1, t, W_MOBA), lambda bi, i: (bi, i, 0)),
        out_shape=jax.ShapeDtypeStruct((b, s, W_MOBA), f32),
        scratch_shapes=[
            pltpu.VMEM((N_MOBA, s, LANES), bf16),
            pltpu.VMEM((N_MOBA // 2, LANES, LANES), f32),
            pltpu.VMEM((N_MOBA, t, LANES), f32),
            pltpu.VMEM((N_MOBA, t, LANES), f32),
        ],
        compiler_params=_cparams(2),
        name="moba_attention",
    )(q_const, q, k, v)


def _alibi_query_lanes(slopes):
    n = slopes.shape[0]
    pieces = _split3(slopes * LOG2E)
    lanes = jnp.arange(LANES)
    rows = []
    for head in range(n):
        rel = lanes - _spare_base(head % 2)
        row = jnp.zeros((LANES,), f32)
        for j in range(2 * N_SPLIT):
            row = jnp.where(rel == j, pieces[j % N_SPLIT][head], row)
        rows.append(row)
    return jnp.stack(rows)


def _dilated_kernel(qc_ref, *refs):
    n_br = len(DIL_PAIRS)
    qkv_refs = [refs[3 * g:3 * g + 3] for g in range(n_br)]
    o_ref, kpos_ref, m_ref, acc_ref = refs[3 * n_br:]
    s_len = o_ref.shape[1]
    blk = DIL_BLOCK
    n_blk = s_len // blk
    lane = _lane_iota((blk, LANES))
    lane_k = _lane_iota((2 * blk, LANES))

    @pl.when((pl.program_id(0) == 0) & (pl.program_id(1) == 0))
    def _():
        ch = 2 * blk
        lane_c = _lane_iota((ch, LANES))
        for g, (_, d) in enumerate(DIL_PAIRS):
            per_res = s_len // d // ch

            def fill(u, carry, g=g, d=d, per_res=per_res):
                res = u // per_res
                sub0 = (u - res * per_res) * ch
                pos = res + d * (sub0 + lax.broadcasted_iota(jnp.int32, (ch, LANES), 0))
                pos_lo = jnp.bitwise_and(pos, blk - 1)
                pos_hi = (pos - pos_lo).astype(f32)
                pos_lo = pos_lo.astype(f32)
                for h in range(2):
                    rel = lane_c - _spare_base(h)
                    val = jnp.where((rel >= 0) & (rel < N_SPLIT), pos_lo,
                                    jnp.where((rel >= N_SPLIT) & (rel < 2 * N_SPLIT), pos_hi, 0.0))
                    kpos_ref[2 * g + h, pl.ds(pl.multiple_of(u * ch, ch), ch), :] = val.astype(bf16)
                return carry

            lax.fori_loop(0, s_len // ch, fill, 0)

    q_const = [jnp.broadcast_to(qc_ref[0, h:h + 1, :], (blk, LANES)).astype(bf16) for h in range(2)]
    iq = lax.broadcasted_iota(jnp.int32, (blk, 2 * blk), 0) + blk
    ik = lax.broadcasted_iota(jnp.int32, (blk, 2 * blk), 1)
    off = iq - ik
    order = sorted(range(n_br), key=lambda g: -DIL_PAIRS[g][1])
    for step, g in enumerate(order):
        window, d = DIL_PAIRS[g]
        q_ref, k_ref, v_ref = qkv_refs[g]
        per_res = n_blk // d
        in_band = (off >= 0) & (off <= window // d)
        first, last = step == 0, step == n_br - 1
        assert not last or d == 1

        def blocks(it, carry, g=g, d=d, per_res=per_res, in_band=in_band, first=first, last=last,
                   q_ref=q_ref, k_ref=k_ref, v_ref=v_ref):
            for u in range(DIL_UNROLL):
                nb = it * DIL_UNROLL + u
                res = nb // per_res
                n = nb - res * per_res
                own = pl.ds(pl.multiple_of(n * blk, blk), blk)
                prev = pl.ds(pl.multiple_of(jnp.maximum(n - 1, 0) * blk, blk), blk)
                flat_own = pl.ds(pl.multiple_of(nb * blk, blk), blk)
                flat_prev = pl.ds(pl.multiple_of(jnp.maximum(nb - 1, 0) * blk, blk), blk)
                q2 = q_ref[0, res, own, :]
                k2 = jnp.concatenate([k_ref[0, res, prev, :], k_ref[0, res, own, :]], axis=0)
                v2 = jnp.concatenate([v_ref[0, res, prev, :], v_ref[0, res, own, :]], axis=0)
                v_aug = _ones_on_other_head(v2)
                first_key = jnp.where(n > 0, 0, blk)
                allowed = in_band & (ik >= first_key)
                if d > 1:
                    nat = pl.ds(res + n * (blk * d), blk, stride=d)
                else:
                    nat = flat_own
                outs = []
                for h in range(2):
                    kp = jnp.concatenate([kpos_ref[2 * g + h, flat_prev, :],
                                          kpos_ref[2 * g + h, flat_own, :]], axis=0)
                    ka = jnp.where(_own_lanes(lane_k, h), k2, kp)
                    qa = jnp.where(_own_lanes(lane, h), q2, q_const[h])
                    s = jnp.where(allowed, _dot_nt(qa, ka), NEG_INF)
                    m_new = jnp.broadcast_to(jnp.max(s, axis=-1, keepdims=True), (blk, LANES))
                    if not first:
                        m_prev = m_ref[h, nat, :]
                        m_new = jnp.maximum(m_prev, m_new)
                    pr = jnp.exp2(s - jnp.concatenate([m_new, m_new], axis=1)).astype(bf16)
                    acc = jnp.dot(pr, v_aug[h], preferred_element_type=f32)
                    if not first:
                        acc = jnp.exp2(m_prev - m_new) * acc_ref[h, nat, :] + acc
                    if last:
                        outs.append(acc)
                    else:
                        m_ref[h, nat, :] = m_new
                        acc_ref[h, nat, :] = acc
                if last:
                    o_ref[0, nat, :] = _finish_pair(outs[0], outs[1])
            return carry

        lax.fori_loop(0, n_blk // DIL_UNROLL, blocks, 0)


def _dilated_attention(q_const, branch_qkv):
    b, _, s, _ = branch_qkv[0][0][0].shape
    n_br = len(DIL_PAIRS)
    assert s % (2 * DIL_BLOCK * DIL_PAIRS[-1][1]) == 0 and (s // DIL_BLOCK) % DIL_UNROLL == 0
    in_specs = [pl.BlockSpec((1, 8, LANES), lambda bi, p: (p, 0, 0))]
    args = [q_const]
    for (qkv, base), (_, d) in zip(branch_qkv, DIL_PAIRS):
        spec = pl.BlockSpec((1, d, s // d, LANES), lambda bi, p, base=base: (bi, 0, 0, base + p))
        in_specs += [spec] * 3
        args += list(qkv)
    return pl.pallas_call(
        _dilated_kernel,
        grid=(b, W_DIL // LANES),
        in_specs=in_specs,
        out_specs=pl.BlockSpec((1, s, LANES), lambda bi, p: (bi, 0, p)),
        out_shape=jax.ShapeDtypeStruct((b, s, W_DIL), f32),
        scratch_shapes=[
            pltpu.VMEM((2 * n_br, s, LANES), bf16),
            pltpu.VMEM((2, s, LANES), f32),
            pltpu.VMEM((2, s, LANES), f32),
        ],
        compiler_params=_cparams(2),
        name="dilated_attention",
    )(*args)


def _dilated_query_lanes(slopes):
    rows = _alibi_query_lanes(slopes)
    pairs = rows.reshape(N_DIL // 2, 2, LANES)
    return jnp.concatenate([pairs, jnp.zeros((N_DIL // 2, 6, LANES), f32)], axis=1)


def _rms(o, gain):
    return o * lax.rsqrt(jnp.mean(o * o, axis=-1, keepdims=True) + EPS) * gain


def _post_kernel(x_ref, of_ref, od_ref, om_ref, gf_ref, gain_ref, wout_ref,
                 g2_ref, wup_ref, wdown_ref, out_ref, y_ref, acc_ref):
    a0, a1 = OFF_DIL, OFF_MOBA
    y_fox = _rms(of_ref[...], gain_ref[:, 0:a0]) * jax.nn.sigmoid(gf_ref[...])
    y_ref[:, 0:a0] = y_fox.astype(bf16)
    y_ref[:, a0:a1] = _rms(od_ref[...], gain_ref[:, a0:a1]).astype(bf16)
    y_ref[:, a1:] = _rms(om_ref[...], gain_ref[:, a1:]).astype(bf16)
    x1 = x_ref[...] + jnp.dot(y_ref[...], wout_ref[...], preferred_element_type=f32)
    hn = _rms(x1, g2_ref[...]).astype(bf16)
    acc_ref[...] = x1
    for c in range(D_FF // QK_CHUNK):
        cols = slice(c * QK_CHUNK, (c + 1) * QK_CHUNK)
        h = jnp.maximum(jnp.dot(hn, wup_ref[:, cols], preferred_element_type=f32), 0.0)
        acc_ref[...] += jnp.dot((h * h).astype(bf16), wdown_ref[cols, :],
                                preferred_element_type=f32)
    out_ref[...] = acc_ref[...]


def _outproj_mlp(x2d, o_fox, o_dil, o_moba, aux, gain, w_out, g2, w_up, w_down):
    m = x2d.shape[0]
    row = lambda i: (i, 0)
    const = lambda i: (0, 0)
    resident = lambda shape: pl.BlockSpec(shape, const, pipeline_mode=pl.Buffered(1))
    return pl.pallas_call(
        _post_kernel,
        grid=(m // ROW_TILE,),
        in_specs=[
            pl.BlockSpec((ROW_TILE, D_MODEL), row),
            pl.BlockSpec((ROW_TILE, W_FOX), row),
            pl.BlockSpec((ROW_TILE, W_DIL), row),
            pl.BlockSpec((ROW_TILE, W_MOBA), row),
            pl.BlockSpec((ROW_TILE, W_FOX), row),
            pl.BlockSpec((1, D_MODEL), const),
            resident((D_MODEL, D_MODEL)),
            pl.BlockSpec((1, D_MODEL), const),
            resident((D_MODEL, D_FF)),
            resident((D_FF, D_MODEL)),
        ],
        out_specs=pl.BlockSpec((ROW_TILE, D_MODEL), row),
        out_shape=jax.ShapeDtypeStruct((m, D_MODEL), f32),
        scratch_shapes=[pltpu.VMEM((ROW_TILE, D_MODEL), bf16), pltpu.VMEM((ROW_TILE, D_MODEL), f32)],
        compiler_params=_cparams(1),
        name="outproj_mlp",
    )(x2d, o_fox, o_dil, o_moba, aux, gain, w_out, g2, w_up, w_down)


def _alibi_slopes():
    return jnp.exp2(-8.0 * jnp.arange(1, N_ALIBI + 1, dtype=f32) / N_ALIBI)


def _head_mean_matrix():
    r = jnp.arange(QK_CHUNK)[:, None] // HEAD_DIM
    c = jnp.arange(QK_CHUNK)[None, :] // HEAD_DIM
    return jnp.where(r == c, 1.0 / HEAD_DIM, 0.0).astype(bf16)


def _permute_families(a, axis):
    moba, fox, dil = jnp.split(a, [W_MOBA, W_MOBA + W_FOX], axis=axis)
    return jnp.concatenate([fox, dil, moba], axis=axis)


def kernel(x, attn_norm, w_in, b_forget, q_gain, k_gain, out_gain, w_out, mlp_norm, w_up, w_down):
    b, s, d = x.shape
    assert d == D_MODEL and s % ATT_TILE == 0 and (b * s) % ROW_TILE == 0
    depth = w_in.shape[0]
    slopes = _alibi_slopes()
    moba_lanes = jnp.concatenate(
        [_alibi_query_lanes(slopes[N_DIL:]), jnp.zeros((8 - N_MOBA, LANES), f32)])
    dil_lanes = _dilated_query_lanes(slopes[:N_DIL])
    gmat = _head_mean_matrix()
    x2d = x.reshape(b * s, d)
    for l in range(depth):
        wqkv, waux = _inproj_weights(w_in, l)
        baux = jnp.concatenate(
            [jnp.zeros((AUX_COLS - N_FOX,), f32), b_forget[l]]).reshape(1, AUX_COLS)
        gain = jnp.concatenate(
            [_permute_families(q_gain[l].reshape(1, d), 1) * (ATTN_SCALE * LOG2E),
             _permute_families(k_gain[l].reshape(1, d), 1)], axis=1)
        q, k, v, aux, *gathered = _inproj(x2d, attn_norm[l].reshape(1, d), wqkv, waux, baux, gain,
                                          gmat, s)
        q3, k3, v3 = (a.reshape(b, s, d) for a in (q, k, v))
        branches = [((q3.reshape(b, 1, s, d), k3.reshape(b, 1, s, d), v3.reshape(b, 1, s, d)),
                     OFF_DIL // LANES)]
        for j in range(len(DIL_GATHER)):
            branches.append((tuple(gathered[t * len(DIL_GATHER) + j] for t in range(3)), 0))
        cum = _fox_scan(aux.reshape(b, s, AUX_COLS))
        o_fox = _fox_attention(q3, k3, v3, cum)
        o_dil = _dilated_attention(dil_lanes, branches)
        o_moba = _moba_attention(moba_lanes, q3, k3, v3)
        x2d = _outproj_mlp(x2d, o_fox.reshape(b * s, W_FOX), o_dil.reshape(b * s, W_DIL),
                           o_moba.reshape(b * s, W_MOBA), aux,
                           _permute_families(out_gain[l].reshape(1, d), 1),
                           _permute_families(w_out[l], 0).astype(bf16),
                           mlp_norm[l].reshape(1, d), w_up[l].astype(bf16), w_down[l].astype(bf16))
    return x2d.reshape(b, s, d)
```

```python
import functools
import math

import jax
import jax.numpy as jnp
from jax import lax
from jax.experimental import pallas as pl
from jax.experimental.pallas import tpu as pltpu

D_MODEL = 1024
HEAD_DIM = 64
N_HEADS = D_MODEL // HEAD_DIM
N_MOBA = N_HEADS // 4
N_FOX = (N_HEADS - N_MOBA) // 2
N_DIL = N_HEADS - N_MOBA - N_FOX
N_ALIBI = N_MOBA + N_DIL
MOBA_BLOCK = 256
MOBA_TOPK = 3
DIL_PAIRS = ((128, 1), (512, 4), (2048, 16))
DIL_BLOCK = 128
D_FF = 4 * D_MODEL
QKV_COLS = 3 * D_MODEL
W_MOBA = N_MOBA * HEAD_DIM
W_FOX = N_FOX * HEAD_DIM
W_DIL = N_DIL * HEAD_DIM
ATTN_SCALE = HEAD_DIM ** -0.5
EPS = 1e-6
NEG_INF = -1e30
LOG2E = math.log2(math.e)

LANES = 128
AUX_COLS = 512
F_COL_BLOCK = AUX_COLS // LANES - 1
F_LANE0 = LANES - N_FOX
VMEM_LIMIT = 56 * 1024 * 1024

ROW_TILE = 512
QK_CHUNK = 512
ATT_TILE = 512
SCAN_TILE = 128
DIL_UNROLL = 16

OFF_FOX, OFF_DIL, OFF_MOBA = 0, W_FOX, W_FOX + W_DIL
MASK_VAL = -(2.0 ** 100)
N_SPLIT = 3

f32 = jnp.float32
bf16 = jnp.bfloat16


def _cparams(n_axes):
    return pltpu.CompilerParams(
        dimension_semantics=("arbitrary",) * n_axes,
        vmem_limit_bytes=VMEM_LIMIT,
    )


def _dot_nt(a, b):
    return lax.dot_general(a, b, (((1,), (1,)), ((), ())), preferred_element_type=f32)


def _lane_iota(shape):
    return lax.broadcasted_iota(jnp.int32, shape, len(shape) - 1)


def _split3(x):
    hi = x.astype(bf16).astype(f32)
    r = x - hi
    mid = r.astype(bf16).astype(f32)
    lo = (r - mid).astype(bf16).astype(f32)
    return hi, mid, lo


def _own_lanes(lane, h):
    return (lane < HEAD_DIM) if h == 0 else (lane >= HEAD_DIM)


def _spare_base(h):
    return HEAD_DIM if h == 0 else 0


def _cast_kernel(w_ref, o_ref):
    o_ref[...] = w_ref[0].astype(bf16)


def _cast_aux_kernel(w_ref, o_ref):
    col = _lane_iota((D_MODEL, AUX_COLS))
    w = jnp.where(col < N_FOX + W_FOX, w_ref[0], 0.0)
    o_ref[...] = pltpu.roll(w, AUX_COLS - N_FOX, 1).astype(bf16)


def _inproj_weights(w_in, l):
    d = D_MODEL
    n_lane_blocks = d // LANES
    moba_blocks = W_MOBA // LANES

    def src_block(j):
        return (j // n_lane_blocks) * n_lane_blocks + (j % n_lane_blocks + moba_blocks) % n_lane_blocks

    wqkv = pl.pallas_call(
        _cast_kernel,
        grid=(QKV_COLS // LANES,),
        in_specs=[pl.BlockSpec((1, d, LANES), lambda j: (l, 0, src_block(j)))],
        out_specs=pl.BlockSpec((d, LANES), lambda j: (0, j)),
        out_shape=jax.ShapeDtypeStruct((d, QKV_COLS), bf16),
        compiler_params=_cparams(1),
        name="cast_qkv",
    )(w_in)
    waux = pl.pallas_call(
        _cast_aux_kernel,
        grid=(1,),
        in_specs=[pl.BlockSpec((1, d, AUX_COLS), lambda j: (l, 0, QKV_COLS // AUX_COLS))],
        out_specs=pl.BlockSpec((d, AUX_COLS), lambda j: (0, 0)),
        out_shape=jax.ShapeDtypeStruct((d, AUX_COLS), bf16),
        compiler_params=_cparams(1),
        name="cast_aux",
    )(w_in)
    return wqkv, waux


DIL_GATHER = tuple(d for _, d in DIL_PAIRS if d > 1)


def _inproj_kernel(x_ref, g_ref, wqkv_ref, waux_ref, baux_ref, gain_ref, gmat_ref,
                   q_ref, k_ref, v_ref, aux_ref, *rest):
    res_refs, stage_ref, stage2_ref = rest[:-2], rest[-2], rest[-1]
    x = x_ref[...]
    ms = jnp.mean(x * x, axis=-1, keepdims=True)
    hn = (x * lax.rsqrt(ms + EPS) * g_ref[...]).astype(bf16)
    n_chunks = QKV_COLS // QK_CHUNK
    per = D_MODEL // QK_CHUNK
    for c in range(n_chunks):
        cols = slice(c * QK_CHUNK, (c + 1) * QK_CHUNK)
        acc = jnp.dot(hn, wqkv_ref[:, cols], preferred_element_type=f32)
        tensor = c // per
        dst = (q_ref, k_ref, v_ref)[tensor]
        chunk0 = (c % per) * QK_CHUNK
        if tensor < 2:
            msq = jnp.dot((acc * acc).astype(bf16), gmat_ref[...], preferred_element_type=f32)
            acc = acc * lax.rsqrt(msq + EPS) * gain_ref[:, cols]
        dst[:, chunk0:chunk0 + QK_CHUNK] = acc.astype(bf16)
        for slab in range(W_DIL // LANES):
            col0 = OFF_DIL + slab * LANES - chunk0
            if 0 <= col0 < QK_CHUNK:
                stage_ref[tensor, slab] = acc[:, col0:col0 + LANES]
    aux_ref[...] = jnp.dot(hn, waux_ref[...], preferred_element_type=f32) + baux_ref[...]
    d1, d2 = DIL_GATHER
    assert d2 == d1 * d1
    for tensor in range(3):
        out1, out2 = res_refs[tensor * 2], res_refs[tensor * 2 + 1]
        for slab in range(W_DIL // LANES):
            cols = slice(slab * LANES, (slab + 1) * LANES)
            for r in range(d1):
                rows = stage_ref[tensor, slab, pl.ds(r, ROW_TILE // d1, stride=d1), :]
                out1[0, r, :, cols] = rows.astype(bf16)
                stage2_ref[tensor, slab, r] = rows
            for r in range(d1):
                for w in range(d1):
                    rows = stage2_ref[tensor, slab, r, pl.ds(w, ROW_TILE // d2, stride=d1), :]
                    out2[0, r + d1 * w, :, cols] = rows.astype(bf16)


def _inproj(x2d, g, wqkv, waux, baux, gain, gmat, seq):
    m = x2d.shape[0]
    tiles_per_seq = seq // ROW_TILE
    row = lambda i: (i, 0)
    const = lambda i: (0, 0)
    res_specs, res_shapes = [], []
    for _ in range(3):
        for d in DIL_GATHER:
            res_specs.append(pl.BlockSpec((1, d, ROW_TILE // d, W_DIL),
                                          lambda i: (i // tiles_per_seq, 0, i % tiles_per_seq, 0)))
            res_shapes.append(jax.ShapeDtypeStruct((m // seq, d, seq // d, W_DIL), bf16))
    return pl.pallas_call(
        _inproj_kernel,
        grid=(m // ROW_TILE,),
        in_specs=[
            pl.BlockSpec((ROW_TILE, D_MODEL), row),
            pl.BlockSpec((1, D_MODEL), const),
            pl.BlockSpec((D_MODEL, QKV_COLS), const),
            pl.BlockSpec((D_MODEL, AUX_COLS), const),
            pl.BlockSpec((1, AUX_COLS), const),
            pl.BlockSpec((1, 2 * D_MODEL), const),
            pl.BlockSpec((QK_CHUNK, QK_CHUNK), const),
        ],
        out_specs=[
            pl.BlockSpec((ROW_TILE, D_MODEL), row),
            pl.BlockSpec((ROW_TILE, D_MODEL), row),
            pl.BlockSpec((ROW_TILE, D_MODEL), row),
            pl.BlockSpec((ROW_TILE, AUX_COLS), row),
        ] + res_specs,
        out_shape=[
            jax.ShapeDtypeStruct((m, D_MODEL), bf16),
            jax.ShapeDtypeStruct((m, D_MODEL), bf16),
            jax.ShapeDtypeStruct((m, D_MODEL), bf16),
            jax.ShapeDtypeStruct((m, AUX_COLS), f32),
        ] + res_shapes,
        scratch_shapes=[
            pltpu.VMEM((3, W_DIL // LANES, ROW_TILE, LANES), f32),
            pltpu.VMEM((3, W_DIL // LANES, DIL_GATHER[0], ROW_TILE // DIL_GATHER[0], LANES), f32),
        ],
        compiler_params=_cparams(1),
        name="inproj",
    )(x2d, g, wqkv, waux, baux, gain, gmat)


def _scan_kernel(f_ref, c_ref):
    n = f_ref.shape[1] // SCAN_TILE
    r = lax.broadcasted_iota(jnp.int32, (SCAN_TILE, SCAN_TILE), 0)
    c = lax.broadcasted_iota(jnp.int32, (SCAN_TILE, SCAN_TILE), 1)
    tri = jnp.where(r >= c, 1.0, 0.0).astype(f32)

    def body(i, carry):
        rows = pl.ds(pl.multiple_of(i * SCAN_TILE, SCAN_TILE), SCAN_TILE)
        f = f_ref[0, rows, :]
        ls = jnp.minimum(f, 0.0) - jnp.log(1.0 + jnp.exp(-jnp.abs(f)))
        cum = jnp.dot(tri, ls, preferred_element_type=f32,
                      precision=lax.Precision.HIGHEST) + carry
        c_ref[0, rows, :] = cum * LOG2E
        return cum[SCAN_TILE - 1:SCAN_TILE, :]

    lax.fori_loop(0, n, body, jnp.zeros((1, LANES), f32))


def _fox_scan(aux3):
    b, s, _ = aux3.shape
    return pl.pallas_call(
        _scan_kernel,
        grid=(b,),
        in_specs=[pl.BlockSpec((1, s, LANES), lambda i: (i, 0, F_COL_BLOCK))],
        out_specs=pl.BlockSpec((1, s, LANES), lambda i: (i, 0, 0)),
        out_shape=jax.ShapeDtypeStruct((b, s, LANES), f32),
        compiler_params=_cparams(1),
        name="fox_scan",
    )(aux3)


def _softmax_step(s, v_aug, m_ref, acc_ref, head, rows=slice(None)):
    m_prev = m_ref[head, rows, :]
    row_max = jnp.max(s, axis=-1, keepdims=True)
    m_new = jnp.maximum(m_prev, jnp.broadcast_to(row_max, m_prev.shape))
    alpha = jnp.exp2(m_prev - m_new)
    m_wide = jnp.concatenate([m_new] * (s.shape[1] // LANES), axis=1)
    p = jnp.exp2(s - m_wide).astype(bf16)
    acc_ref[head, rows, :] = (alpha * acc_ref[head, rows, :]
                              + jnp.dot(p, v_aug, preferred_element_type=f32))
    m_ref[head, rows, :] = m_new


def _ones_on_other_head(v2):
    first = _lane_iota(v2.shape) < HEAD_DIM
    one = jnp.ones_like(v2)
    return jnp.where(first, v2, one), jnp.where(first, one, v2)


def _finish_pair(acc_a, acc_b):
    o_a = acc_a / pltpu.roll(acc_a, HEAD_DIM, 1)
    o_b = acc_b / pltpu.roll(acc_b, HEAD_DIM, 1)
    return jnp.where(_lane_iota(acc_a.shape) < HEAD_DIM, o_a, o_b)


def _causal_chunk_then_past(i, n_heads, scores, values, write, m_ref, acc_ref):
    t = ATT_TILE
    m_ref[...] = jnp.full(m_ref.shape, NEG_INF, f32)
    acc_ref[...] = jnp.zeros(acc_ref.shape, f32)

    def chunk(j, causal):
        rows = pl.ds(pl.multiple_of(j * t, t), t)
        for p in range(n_heads // 2):
            v_aug = _ones_on_other_head(values(rows, p))
            for h in range(2):
                head = 2 * p + h
                s = scores(head, slice(None), rows)
                if causal:
                    r = lax.broadcasted_iota(jnp.int32, (t, t), 0)
                    c = lax.broadcasted_iota(jnp.int32, (t, t), 1)
                    s = jnp.where(c <= r, s, NEG_INF)
                _softmax_step(s, v_aug[h], m_ref, acc_ref, head)

    chunk(i, True)

    def body(j, carry):
        chunk(j, False)
        return carry

    lax.fori_loop(0, i, body, 0)
    for p in range(n_heads // 2):
        write(p, _finish_pair(acc_ref[2 * p], acc_ref[2 * p + 1]))


def _fox_keys_and_queries(i, q_ref, k_ref, c_ref, kaug_ref):
    t = ATT_TILE
    s_len = k_ref.shape[1]
    lane = _lane_iota((t, LANES))

    @pl.when(i == 0)
    def _():
        def build(r, carry):
            rows = pl.ds(pl.multiple_of(r * t, t), t)
            c = c_ref[0, rows, :]
            for p in range(N_FOX // 2):
                k2 = k_ref[0, rows, p * LANES:(p + 1) * LANES].astype(f32)
                for h in range(2):
                    head = 2 * p + h
                    base = _spare_base(h)
                    fl = F_LANE0 + head
                    pieces = _split3(-jnp.broadcast_to(c[:, fl:fl + 1], (t, LANES)))
                    ka = jnp.where(_own_lanes(lane, h), k2, 0.0)
                    for n, piece in enumerate(pieces):
                        ka = jnp.where(lane == base + n, piece, ka)
                    kaug_ref[head, rows, :] = ka.astype(bf16)
            return carry
        lax.fori_loop(0, s_len // t, build, 0)

    q_heads = []
    for p in range(N_FOX // 2):
        q2 = q_ref[0, :, p * LANES:(p + 1) * LANES].astype(f32)
        for h in range(2):
            base = _spare_base(h)
            ones = jnp.where((lane >= base) & (lane < base + N_SPLIT), 1.0, 0.0)
            q_heads.append(jnp.where(_own_lanes(lane, h), q2, ones).astype(bf16))
    return q_heads


def _moba_keys_and_queries(i, qc_ref, q_ref, k_ref, kaug_ref, kmean_ref, head0):
    t = ATT_TILE
    bs = MOBA_BLOCK
    s_len = k_ref.shape[1]
    n_blk = s_len // bs
    per_tile = t // bs
    lane = _lane_iota((t, LANES))
    mask_lane0 = 2 * N_SPLIT

    @pl.when(i == 0)
    def _():
        kmean_ref[...] = jnp.zeros(kmean_ref.shape, f32)

        def build(r, carry):
            rows = pl.ds(pl.multiple_of(r * t, t), t)
            pos = r * t + lax.broadcasted_iota(jnp.int32, (t, LANES), 0)
            blk = pos // bs
            in_blk = (pos - blk * bs).astype(f32)
            blk_start = (blk * bs).astype(f32)
            for p in range(N_MOBA // 2):
                k2 = k_ref[0, rows, p * LANES:(p + 1) * LANES].astype(f32)
                for u in range(per_tile):
                    kmean_ref[p, pl.ds(r * per_tile + u, 1), :] = jnp.mean(
                        k2[u * bs:(u + 1) * bs], axis=0, keepdims=True)
                for h in range(2):
                    base = _spare_base(h)
                    rel = lane - base
                    ka = jnp.where(_own_lanes(lane, h), k2, 0.0)
                    ka = jnp.where((rel >= 0) & (rel < N_SPLIT), in_blk, ka)
                    ka = jnp.where((rel >= N_SPLIT) & (rel < mask_lane0), blk_start, ka)
                    ka = jnp.where(rel - mask_lane0 == blk, MASK_VAL, ka)
                    kaug_ref[head0 + 2 * p + h, rows, :] = ka.astype(bf16)
            return carry
        lax.fori_loop(0, s_len // t, build, 0)

    sub = lax.broadcasted_iota(jnp.int32, (n_blk, t), 0)
    q_blk = i * per_tile + lax.broadcasted_iota(jnp.int32, (1, t), 1) // bs
    q_heads = []
    for p in range(N_MOBA // 2):
        q2 = q_ref[0, :, p * LANES:(p + 1) * LANES].astype(f32)
        for h in range(2):
            head = 2 * p + h
            q_own = jnp.where(_own_lanes(lane, h), q2, 0.0)
            gate = lax.dot_general(kmean_ref[p, 0:n_blk, :], q_own, (((1,), (1,)), ((), ())),
                                   preferred_element_type=f32,
                                   precision=lax.Precision.HIGHEST)
            rank = jnp.zeros((n_blk, t), f32)
            for j in range(n_blk):
                gj = gate[j:j + 1, :]
                beats = (gj > gate) | ((gj == gate) & (j < sub))
                rank = rank + jnp.where(beats & (j < q_blk), 1.0, 0.0)
            dropped = jnp.where((sub < q_blk) & (rank >= MOBA_TOPK), 1.0, 0.0)
            dropped = jnp.concatenate([dropped, jnp.zeros((LANES - n_blk, t), f32)], axis=0)
            dropped = pltpu.roll(dropped.T, _spare_base(h) + mask_lane0, 1)
            q_heads.append((q_own + qc_ref[head:head + 1, :] + dropped).astype(bf16))
    return q_heads


def _fox_moba_kernel(qc_ref, qf_ref, kf_ref, vf_ref, c_ref, qm_ref, km_ref, vm_ref,
                     of_ref, om_ref, kaug_ref, kmean_ref, m_ref, acc_ref):
    i = pl.program_id(1)
    q_heads = _fox_keys_and_queries(i, qf_ref, kf_ref, c_ref, kaug_ref)
    q_heads += _moba_keys_and_queries(i, qc_ref, qm_ref, km_ref, kaug_ref, kmean_ref, N_FOX)
    fox_pairs = N_FOX // 2

    def values(rows, p):
        if p < fox_pairs:
            return vf_ref[0, rows, p * LANES:(p + 1) * LANES]
        return vm_ref[0, rows, (p - fox_pairs) * LANES:(p - fox_pairs + 1) * LANES]

    def write(p, o):
        if p < fox_pairs:
            of_ref[0, :, p * LANES:(p + 1) * LANES] = o
        else:
            om_ref[0, :, (p - fox_pairs) * LANES:(p - fox_pairs + 1) * LANES] = o

    _causal_chunk_then_past(
        i, N_FOX + N_MOBA,
        lambda head, q_rows, k_rows: _dot_nt(q_heads[head][q_rows], kaug_ref[head, k_rows, :]),
        values, write, m_ref, acc_ref)


def _fox_moba_attention(moba_const, q, k, v, cum):
    b, s, _ = q.shape
    t = ATT_TILE
    fox_blk, moba_blk = OFF_FOX // W_FOX, OFF_MOBA // W_MOBA
    n_heads = N_FOX + N_MOBA
    assert 2 * N_SPLIT + s // MOBA_BLOCK <= HEAD_DIM and s // MOBA_BLOCK <= LANES
    tile = lambda width, blk: pl.BlockSpec((1, t, width), lambda bi, i: (bi, i, blk))
    whole = lambda width, blk: pl.BlockSpec((1, s, width), lambda bi, i: (bi, 0, blk))
    return pl.pallas_call(
        _fox_moba_kernel,
        grid=(b, s // t),
        in_specs=[
            pl.BlockSpec((8, LANES), lambda bi, i: (0, 0)),
            tile(W_FOX, fox_blk), whole(W_FOX, fox_blk), whole(W_FOX, fox_blk),
            whole(LANES, 0),
            tile(W_MOBA, moba_blk), whole(W_MOBA, moba_blk), whole(W_MOBA, moba_blk),
        ],
        out_specs=[tile(W_FOX, 0), tile(W_MOBA, 0)],
        out_shape=[jax.ShapeDtypeStruct((b, s, W_FOX), f32),
                   jax.ShapeDtypeStruct((b, s, W_MOBA), f32)],
        scratch_shapes=[
            pltpu.VMEM((n_heads, s, LANES), bf16),
            pltpu.VMEM((N_MOBA // 2, LANES, LANES), f32),
            pltpu.VMEM((n_heads, t, LANES), f32),
            pltpu.VMEM((n_heads, t, LANES), f32),
        ],
        compiler_params=_cparams(2),
        name="fox_moba_attention",
    )(moba_const, q, k, v, cum, q, k, v)


def _alibi_query_lanes(slopes):
    n = slopes.shape[0]
    pieces = _split3(slopes * LOG2E)
    lanes = jnp.arange(LANES)
    rows = []
    for head in range(n):
        rel = lanes - _spare_base(head % 2)
        row = jnp.zeros((LANES,), f32)
        for j in range(2 * N_SPLIT):
            row = jnp.where(rel == j, pieces[j % N_SPLIT][head], row)
        rows.append(row)
    return jnp.stack(rows)


def _dilated_kernel(qc_ref, *refs):
    n_br = len(DIL_PAIRS)
    qkv_refs = [refs[3 * g:3 * g + 3] for g in range(n_br)]
    o_ref, kpos_ref, m_ref, acc_ref = refs[3 * n_br:]
    s_len = o_ref.shape[1]
    blk = DIL_BLOCK
    n_blk = s_len // blk
    lane = _lane_iota((blk, LANES))
    lane_k = _lane_iota((2 * blk, LANES))

    @pl.when((pl.program_id(0) == 0) & (pl.program_id(1) == 0))
    def _():
        ch = 2 * blk
        lane_c = _lane_iota((ch, LANES))
        for g, (_, d) in enumerate(DIL_PAIRS):
            per_res = s_len // d // ch

            def fill(u, carry, g=g, d=d, per_res=per_res):
                res = u // per_res
                sub0 = (u - res * per_res) * ch
                pos = res + d * (sub0 + lax.broadcasted_iota(jnp.int32, (ch, LANES), 0))
                pos_lo = jnp.bitwise_and(pos, blk - 1)
                pos_hi = (pos - pos_lo).astype(f32)
                pos_lo = pos_lo.astype(f32)
                for h in range(2):
                    rel = lane_c - _spare_base(h)
                    val = jnp.where((rel >= 0) & (rel < N_SPLIT), pos_lo,
                                    jnp.where((rel >= N_SPLIT) & (rel < 2 * N_SPLIT), pos_hi, 0.0))
                    kpos_ref[2 * g + h, pl.ds(pl.multiple_of(u * ch, ch), ch), :] = val.astype(bf16)
                return carry

            lax.fori_loop(0, s_len // ch, fill, 0)

    q_const = [jnp.broadcast_to(qc_ref[0, h:h + 1, :], (blk, LANES)).astype(bf16) for h in range(2)]
    iq = lax.broadcasted_iota(jnp.int32, (blk, 2 * blk), 0) + blk
    ik = lax.broadcasted_iota(jnp.int32, (blk, 2 * blk), 1)
    off = iq - ik
    order = sorted(range(n_br), key=lambda g: -DIL_PAIRS[g][1])
    for step, g in enumerate(order):
        window, d = DIL_PAIRS[g]
        q_ref, k_ref, v_ref = qkv_refs[g]
        per_res = n_blk // d
        in_band = (off >= 0) & (off <= window // d)
        first, last = step == 0, step == n_br - 1
        assert not last or d == 1

        unroll = DIL_UNROLL // 2 if last else DIL_UNROLL

        def blocks(it, carry, g=g, d=d, per_res=per_res, in_band=in_band, first=first, last=last,
                   q_ref=q_ref, k_ref=k_ref, v_ref=v_ref, unroll=unroll):
            for u in range(unroll):
                nb = it * unroll + u
                res = nb // per_res
                n = nb - res * per_res
                own = pl.ds(pl.multiple_of(n * blk, blk), blk)
                prev = pl.ds(pl.multiple_of(jnp.maximum(n - 1, 0) * blk, blk), blk)
                flat_own = pl.ds(pl.multiple_of(nb * blk, blk), blk)
                flat_prev = pl.ds(pl.multiple_of(jnp.maximum(nb - 1, 0) * blk, blk), blk)
                q2 = q_ref[0, res, own, :]
                k2 = jnp.concatenate([k_ref[0, res, prev, :], k_ref[0, res, own, :]], axis=0)
                v2 = jnp.concatenate([v_ref[0, res, prev, :], v_ref[0, res, own, :]], axis=0)
                v_aug = _ones_on_other_head(v2)
                first_key = jnp.where(n > 0, 0, blk)
                allowed = in_band & (ik >= first_key)
                if d > 1:
                    nat = pl.ds(res + n * (blk * d), blk, stride=d)
                else:
                    nat = flat_own
                outs = []
                for h in range(2):
                    kp = jnp.concatenate([kpos_ref[2 * g + h, flat_prev, :],
                                          kpos_ref[2 * g + h, flat_own, :]], axis=0)
                    ka = jnp.where(_own_lanes(lane_k, h), k2, kp)
                    qa = jnp.where(_own_lanes(lane, h), q2, q_const[h])
                    s = jnp.where(allowed, _dot_nt(qa, ka), NEG_INF)
                    m_new = jnp.broadcast_to(jnp.max(s, axis=-1, keepdims=True), (blk, LANES))
                    if not first:
                        m_prev = m_ref[h, nat, :]
                        m_new = jnp.maximum(m_prev, m_new)
                    pr = jnp.exp2(s - jnp.concatenate([m_new, m_new], axis=1)).astype(bf16)
                    acc = jnp.dot(pr, v_aug[h], preferred_element_type=f32)
                    if not first:
                        acc = jnp.exp2(m_prev - m_new) * acc_ref[h, nat, :] + acc
                    if last:
                        outs.append(acc)
                    else:
                        m_ref[h, nat, :] = m_new
                        acc_ref[h, nat, :] = acc
                if last:
                    o_ref[0, nat, :] = _finish_pair(outs[0], outs[1])
            return carry

        lax.fori_loop(0, n_blk // unroll, blocks, 0)


def _dilated_attention(q_const, branch_qkv):
    b, _, s, _ = branch_qkv[0][0][0].shape
    n_br = len(DIL_PAIRS)
    assert s % (2 * DIL_BLOCK * DIL_PAIRS[-1][1]) == 0 and (s // DIL_BLOCK) % DIL_UNROLL == 0
    in_specs = [pl.BlockSpec((1, 8, LANES), lambda bi, p: (p, 0, 0))]
    args = [q_const]
    for (qkv, base), (_, d) in zip(branch_qkv, DIL_PAIRS):
        spec = pl.BlockSpec((1, d, s // d, LANES), lambda bi, p, base=base: (bi, 0, 0, base + p))
        in_specs += [spec] * 3
        args += list(qkv)
    return pl.pallas_call(
        _dilated_kernel,
        grid=(b, W_DIL // LANES),
        in_specs=in_specs,
        out_specs=pl.BlockSpec((1, s, LANES), lambda bi, p: (bi, 0, p)),
        out_shape=jax.ShapeDtypeStruct((b, s, W_DIL), f32),
        scratch_shapes=[
            pltpu.VMEM((2 * n_br, s, LANES), bf16),
            pltpu.VMEM((2, s, LANES), f32),
            pltpu.VMEM((2, s, LANES), f32),
        ],
        compiler_params=_cparams(2),
        name="dilated_attention",
    )(*args)


def _dilated_query_lanes(slopes):
    rows = _alibi_query_lanes(slopes)
    pairs = rows.reshape(N_DIL // 2, 2, LANES)
    return jnp.concatenate([pairs, jnp.zeros((N_DIL // 2, 6, LANES), f32)], axis=1)


def _rms(o, gain):
    return o * lax.rsqrt(jnp.mean(o * o, axis=-1, keepdims=True) + EPS) * gain


def _post_kernel(x_ref, of_ref, od_ref, om_ref, gf_ref, gain_ref, wout_ref,
                 g2_ref, wup_ref, wdown_ref, out_ref, y_ref, acc_ref):
    a0, a1 = OFF_DIL, OFF_MOBA
    y_fox = _rms(of_ref[...], gain_ref[:, 0:a0]) * jax.nn.sigmoid(gf_ref[...])
    y_ref[:, 0:a0] = y_fox.astype(bf16)
    y_ref[:, a0:a1] = _rms(od_ref[...], gain_ref[:, a0:a1]).astype(bf16)
    y_ref[:, a1:] = _rms(om_ref[...], gain_ref[:, a1:]).astype(bf16)
    x1 = x_ref[...] + jnp.dot(y_ref[...], wout_ref[...], preferred_element_type=f32)
    hn = _rms(x1, g2_ref[...]).astype(bf16)
    acc_ref[...] = x1
    for c in range(D_FF // QK_CHUNK):
        cols = slice(c * QK_CHUNK, (c + 1) * QK_CHUNK)
        h = jnp.maximum(jnp.dot(hn, wup_ref[:, cols], preferred_element_type=f32), 0.0)
        acc_ref[...] += jnp.dot((h * h).astype(bf16), wdown_ref[cols, :],
                                preferred_element_type=f32)
    out_ref[...] = acc_ref[...]


def _outproj_mlp(x2d, o_fox, o_dil, o_moba, aux, gain, w_out, g2, w_up, w_down):
    m = x2d.shape[0]
    row = lambda i: (i, 0)
    const = lambda i: (0, 0)
    resident = lambda shape: pl.BlockSpec(shape, const, pipeline_mode=pl.Buffered(1))
    return pl.pallas_call(
        _post_kernel,
        grid=(m // ROW_TILE,),
        in_specs=[
            pl.BlockSpec((ROW_TILE, D_MODEL), row),
            pl.BlockSpec((ROW_TILE, W_FOX), row),
            pl.BlockSpec((ROW_TILE, W_DIL), row),
            pl.BlockSpec((ROW_TILE, W_MOBA), row),
            pl.BlockSpec((ROW_TILE, W_FOX), row),
            pl.BlockSpec((1, D_MODEL), const),
            resident((D_MODEL, D_MODEL)),
            pl.BlockSpec((1, D_MODEL), const),
            resident((D_MODEL, D_FF)),
            resident((D_FF, D_MODEL)),
        ],
        out_specs=pl.BlockSpec((ROW_TILE, D_MODEL), row),
        out_shape=jax.ShapeDtypeStruct((m, D_MODEL), f32),
        scratch_shapes=[pltpu.VMEM((ROW_TILE, D_MODEL), bf16), pltpu.VMEM((ROW_TILE, D_MODEL), f32)],
        compiler_params=_cparams(1),
        name="outproj_mlp",
    )(x2d, o_fox, o_dil, o_moba, aux, gain, w_out, g2, w_up, w_down)


def _alibi_slopes():
    return jnp.exp2(-8.0 * jnp.arange(1, N_ALIBI + 1, dtype=f32) / N_ALIBI)


def _head_mean_matrix():
    r = jnp.arange(QK_CHUNK)[:, None] // HEAD_DIM
    c = jnp.arange(QK_CHUNK)[None, :] // HEAD_DIM
    return jnp.where(r == c, 1.0 / HEAD_DIM, 0.0).astype(bf16)


def _permute_families(a, axis):
    moba, fox, dil = jnp.split(a, [W_MOBA, W_MOBA + W_FOX], axis=axis)
    return jnp.concatenate([fox, dil, moba], axis=axis)


def kernel(x, attn_norm, w_in, b_forget, q_gain, k_gain, out_gain, w_out, mlp_norm, w_up, w_down):
    b, s, d = x.shape
    assert d == D_MODEL and s % ATT_TILE == 0 and (b * s) % ROW_TILE == 0
    depth = w_in.shape[0]
    slopes = _alibi_slopes()
    moba_lanes = jnp.concatenate(
        [_alibi_query_lanes(slopes[N_DIL:]), jnp.zeros((8 - N_MOBA, LANES), f32)])
    dil_lanes = _dilated_query_lanes(slopes[:N_DIL])
    gmat = _head_mean_matrix()
    x2d = x.reshape(b * s, d)
    for l in range(depth):
        wqkv, waux = _inproj_weights(w_in, l)
        baux = jnp.concatenate(
            [jnp.zeros((AUX_COLS - N_FOX,), f32), b_forget[l]]).reshape(1, AUX_COLS)
        gain = jnp.concatenate(
            [_permute_families(q_gain[l].reshape(1, d), 1) * (ATTN_SCALE * LOG2E),
             _permute_families(k_gain[l].reshape(1, d), 1)], axis=1)
        q, k, v, aux, *gathered = _inproj(x2d, attn_norm[l].reshape(1, d), wqkv, waux, baux, gain,
                                          gmat, s)
        q3, k3, v3 = (a.reshape(b, s, d) for a in (q, k, v))
        branches = [((q3.reshape(b, 1, s, d), k3.reshape(b, 1, s, d), v3.reshape(b, 1, s, d)),
                     OFF_DIL // LANES)]
        for j in range(len(DIL_GATHER)):
            branches.append((tuple(gathered[t * len(DIL_GATHER) + j] for t in range(3)), 0))
        cum = _fox_scan(aux.reshape(b, s, AUX_COLS))
        o_fox, o_moba = _fox_moba_attention(moba_lanes, q3, k3, v3, cum)
        o_dil = _dilated_attention(dil_lanes, branches)
        x2d = _outproj_mlp(x2d, o_fox.reshape(b * s, W_FOX), o_dil.reshape(b * s, W_DIL),
                           o_moba.reshape(b * s, W_MOBA), aux,
                           _permute_families(out_gain[l].reshape(1, d), 1),
                           _permute_families(w_out[l], 0).astype(bf16),
                           mlp_norm[l].reshape(1, d), w_up[l].astype(bf16), w_down[l].astype(bf16))
    return x2d.reshape(b, s, d)
```

```python
import functools
import math

import jax
import jax.numpy as jnp
from jax import lax
from jax.experimental import pallas as pl
from jax.experimental.pallas import tpu as pltpu

D_MODEL = 1024
HEAD_DIM = 64
N_HEADS = D_MODEL // HEAD_DIM
N_MOBA = N_HEADS // 4
N_FOX = (N_HEADS - N_MOBA) // 2
N_DIL = N_HEADS - N_MOBA - N_FOX
N_ALIBI = N_MOBA + N_DIL
MOBA_BLOCK = 256
MOBA_TOPK = 3
DIL_PAIRS = ((128, 1), (512, 4), (2048, 16))
DIL_BLOCK = 128
D_FF = 4 * D_MODEL
QKV_COLS = 3 * D_MODEL
W_MOBA = N_MOBA * HEAD_DIM
W_FOX = N_FOX * HEAD_DIM
W_DIL = N_DIL * HEAD_DIM
ATTN_SCALE = HEAD_DIM ** -0.5
EPS = 1e-6
NEG_INF = -1e30
LOG2E = math.log2(math.e)

LANES = 128
AUX_COLS = 512
F_COL_BLOCK = AUX_COLS // LANES - 1
F_LANE0 = LANES - N_FOX
VMEM_LIMIT = 56 * 1024 * 1024

ROW_TILE = 512
QK_CHUNK = 512
ATT_TILE = 512
SCAN_TILE = 128
SCAN_BATCH = 4
DIL_UNROLL = 16

OFF_FOX, OFF_DIL, OFF_MOBA = 0, W_FOX, W_FOX + W_DIL
MASK_VAL = -(2.0 ** 100)
N_SPLIT = 3

f32 = jnp.float32
bf16 = jnp.bfloat16


def _cparams(n_axes):
    return pltpu.CompilerParams(
        dimension_semantics=("arbitrary",) * n_axes,
        vmem_limit_bytes=VMEM_LIMIT,
    )


def _dot_nt(a, b):
    return lax.dot_general(a, b, (((1,), (1,)), ((), ())), preferred_element_type=f32)


def _lane_iota(shape):
    return lax.broadcasted_iota(jnp.int32, shape, len(shape) - 1)


def _split3(x):
    hi = x.astype(bf16).astype(f32)
    r = x - hi
    mid = r.astype(bf16).astype(f32)
    lo = (r - mid).astype(bf16).astype(f32)
    return hi, mid, lo


def _own_lanes(lane, h):
    return (lane < HEAD_DIM) if h == 0 else (lane >= HEAD_DIM)


def _spare_base(h):
    return HEAD_DIM if h == 0 else 0


def _cast_kernel(w_ref, o_ref):
    o_ref[...] = w_ref[0].astype(bf16)


def _cast_aux_kernel(w_ref, o_ref):
    col = _lane_iota((D_MODEL, AUX_COLS))
    w = jnp.where(col < N_FOX + W_FOX, w_ref[0], 0.0)
    o_ref[...] = pltpu.roll(w, AUX_COLS - N_FOX, 1).astype(bf16)


def _inproj_weights(w_in, l):
    d = D_MODEL
    n_lane_blocks = d // LANES
    moba_blocks = W_MOBA // LANES

    def src_block(j):
        return (j // n_lane_blocks) * n_lane_blocks + (j % n_lane_blocks + moba_blocks) % n_lane_blocks

    wqkv = pl.pallas_call(
        _cast_kernel,
        grid=(QKV_COLS // LANES,),
        in_specs=[pl.BlockSpec((1, d, LANES), lambda j: (l, 0, src_block(j)))],
        out_specs=pl.BlockSpec((d, LANES), lambda j: (0, j)),
        out_shape=jax.ShapeDtypeStruct((d, QKV_COLS), bf16),
        compiler_params=_cparams(1),
        name="cast_qkv",
    )(w_in)
    waux = pl.pallas_call(
        _cast_aux_kernel,
        grid=(1,),
        in_specs=[pl.BlockSpec((1, d, AUX_COLS), lambda j: (l, 0, QKV_COLS // AUX_COLS))],
        out_specs=pl.BlockSpec((d, AUX_COLS), lambda j: (0, 0)),
        out_shape=jax.ShapeDtypeStruct((d, AUX_COLS), bf16),
        compiler_params=_cparams(1),
        name="cast_aux",
    )(w_in)
    return wqkv, waux


DIL_GATHER = tuple(d for _, d in DIL_PAIRS if d > 1)


def _inproj_kernel(x_ref, g_ref, wqkv_ref, waux_ref, baux_ref, gain_ref, gmat_ref,
                   q_ref, k_ref, v_ref, aux_ref, *rest):
    res_refs, stage_ref, stage2_ref = rest[:-2], rest[-2], rest[-1]
    x = x_ref[...]
    ms = jnp.mean(x * x, axis=-1, keepdims=True)
    hn = (x * lax.rsqrt(ms + EPS) * g_ref[...]).astype(bf16)
    n_chunks = QKV_COLS // QK_CHUNK
    per = D_MODEL // QK_CHUNK
    for c in range(n_chunks):
        cols = slice(c * QK_CHUNK, (c + 1) * QK_CHUNK)
        acc = jnp.dot(hn, wqkv_ref[:, cols], preferred_element_type=f32)
        tensor = c // per
        dst = (q_ref, k_ref, v_ref)[tensor]
        chunk0 = (c % per) * QK_CHUNK
        if tensor < 2:
            msq = jnp.dot((acc * acc).astype(bf16), gmat_ref[...], preferred_element_type=f32)
            acc = acc * lax.rsqrt(msq + EPS) * gain_ref[:, cols]
        dst[:, chunk0:chunk0 + QK_CHUNK] = acc.astype(bf16)
        for slab in range(W_DIL // LANES):
            col0 = OFF_DIL + slab * LANES - chunk0
            if 0 <= col0 < QK_CHUNK:
                stage_ref[tensor, slab] = acc[:, col0:col0 + LANES]
    aux_ref[...] = jnp.dot(hn, waux_ref[...], preferred_element_type=f32) + baux_ref[...]
    d1, d2 = DIL_GATHER
    assert d2 == d1 * d1
    for tensor in range(3):
        out1, out2 = res_refs[tensor * 2], res_refs[tensor * 2 + 1]
        for slab in range(W_DIL // LANES):
            cols = slice(slab * LANES, (slab + 1) * LANES)
            for r in range(d1):
                rows = stage_ref[tensor, slab, pl.ds(r, ROW_TILE // d1, stride=d1), :]
                out1[0, r, :, cols] = rows.astype(bf16)
                stage2_ref[tensor, slab, r] = rows
            for r in range(d1):
                for w in range(d1):
                    rows = stage2_ref[tensor, slab, r, pl.ds(w, ROW_TILE // d2, stride=d1), :]
                    out2[0, r + d1 * w, :, cols] = rows.astype(bf16)


def _inproj(x2d, g, wqkv, waux, baux, gain, gmat, seq):
    m = x2d.shape[0]
    tiles_per_seq = seq // ROW_TILE
    row = lambda i: (i, 0)
    const = lambda i: (0, 0)
    res_specs, res_shapes = [], []
    for _ in range(3):
        for d in DIL_GATHER:
            res_specs.append(pl.BlockSpec((1, d, ROW_TILE // d, W_DIL),
                                          lambda i: (i // tiles_per_seq, 0, i % tiles_per_seq, 0)))
            res_shapes.append(jax.ShapeDtypeStruct((m // seq, d, seq // d, W_DIL), bf16))
    return pl.pallas_call(
        _inproj_kernel,
        grid=(m // ROW_TILE,),
        in_specs=[
            pl.BlockSpec((ROW_TILE, D_MODEL), row),
            pl.BlockSpec((1, D_MODEL), const),
            pl.BlockSpec((D_MODEL, QKV_COLS), const),
            pl.BlockSpec((D_MODEL, AUX_COLS), const),
            pl.BlockSpec((1, AUX_COLS), const),
            pl.BlockSpec((1, 2 * D_MODEL), const),
            pl.BlockSpec((QK_CHUNK, QK_CHUNK), const),
        ],
        out_specs=[
            pl.BlockSpec((ROW_TILE, D_MODEL), row),
            pl.BlockSpec((ROW_TILE, D_MODEL), row),
            pl.BlockSpec((ROW_TILE, D_MODEL), row),
            pl.BlockSpec((ROW_TILE, AUX_COLS), row),
        ] + res_specs,
        out_shape=[
            jax.ShapeDtypeStruct((m, D_MODEL), bf16),
            jax.ShapeDtypeStruct((m, D_MODEL), bf16),
            jax.ShapeDtypeStruct((m, D_MODEL), bf16),
            jax.ShapeDtypeStruct((m, AUX_COLS), f32),
        ] + res_shapes,
        scratch_shapes=[
            pltpu.VMEM((3, W_DIL // LANES, ROW_TILE, LANES), f32),
            pltpu.VMEM((3, W_DIL // LANES, DIL_GATHER[0], ROW_TILE // DIL_GATHER[0], LANES), f32),
        ],
        compiler_params=_cparams(1),
        name="inproj",
    )(x2d, g, wqkv, waux, baux, gain, gmat)


def _scan_kernel(f_ref, c_ref):
    n_batch = f_ref.shape[0]
    n = f_ref.shape[1] // SCAN_TILE
    r = lax.broadcasted_iota(jnp.int32, (SCAN_TILE, SCAN_TILE), 0)
    c = lax.broadcasted_iota(jnp.int32, (SCAN_TILE, SCAN_TILE), 1)
    tri = jnp.where(r >= c, 1.0, 0.0).astype(bf16)

    def body(i, carry):
        rows = pl.ds(pl.multiple_of(i * SCAN_TILE, SCAN_TILE), SCAN_TILE)
        last = []
        for bi in range(n_batch):
            f = f_ref[bi, rows, :]
            ls = jnp.minimum(f, 0.0) - jnp.log(1.0 + jnp.exp(-jnp.abs(f)))
            pieces = jnp.concatenate(_split3(ls), axis=1).astype(bf16)
            part = jnp.dot(tri, pieces, preferred_element_type=f32)
            cum = (part[:, :LANES] + part[:, LANES:2 * LANES] + part[:, 2 * LANES:]) + carry[bi]
            c_ref[bi, rows, :] = cum * LOG2E
            last.append(cum[SCAN_TILE - 1:SCAN_TILE, :])
        return tuple(last)

    lax.fori_loop(0, n, body, tuple(jnp.zeros((1, LANES), f32) for _ in range(n_batch)))


def _fox_scan(aux3):
    b, s, _ = aux3.shape
    per_step = math.gcd(b, SCAN_BATCH)
    return pl.pallas_call(
        _scan_kernel,
        grid=(b // per_step,),
        in_specs=[pl.BlockSpec((per_step, s, LANES), lambda i: (i, 0, F_COL_BLOCK))],
        out_specs=pl.BlockSpec((per_step, s, LANES), lambda i: (i, 0, 0)),
        out_shape=jax.ShapeDtypeStruct((b, s, LANES), f32),
        compiler_params=_cparams(1),
        name="fox_scan",
    )(aux3)


def _softmax_step(s, v_aug, m_ref, acc_ref, head, rows=slice(None)):
    m_prev = m_ref[head, rows, :]
    row_max = jnp.max(s, axis=-1, keepdims=True)
    m_new = jnp.maximum(m_prev, jnp.broadcast_to(row_max, m_prev.shape))
    alpha = jnp.exp2(m_prev - m_new)
    m_wide = jnp.concatenate([m_new] * (s.shape[1] // LANES), axis=1)
    p = jnp.exp2(s - m_wide).astype(bf16)
    acc_ref[head, rows, :] = (alpha * acc_ref[head, rows, :]
                              + jnp.dot(p, v_aug, preferred_element_type=f32))
    m_ref[head, rows, :] = m_new


def _ones_on_other_head(v2):
    first = _lane_iota(v2.shape) < HEAD_DIM
    one = jnp.ones_like(v2)
    return jnp.where(first, v2, one), jnp.where(first, one, v2)


def _finish_pair(acc_a, acc_b):
    o_a = acc_a / pltpu.roll(acc_a, HEAD_DIM, 1)
    o_b = acc_b / pltpu.roll(acc_b, HEAD_DIM, 1)
    return jnp.where(_lane_iota(acc_a.shape) < HEAD_DIM, o_a, o_b)


def _causal_chunk_then_past(i, n_heads, scores, values, write, m_ref, acc_ref):
    t = ATT_TILE
    m_ref[...] = jnp.full(m_ref.shape, NEG_INF, f32)
    acc_ref[...] = jnp.zeros(acc_ref.shape, f32)

    def chunk(j, causal):
        rows = pl.ds(pl.multiple_of(j * t, t), t)
        for p in range(n_heads // 2):
            v_aug = _ones_on_other_head(values(rows, p))
            for h in range(2):
                head = 2 * p + h
                s = scores(head, slice(None), rows)
                if causal:
                    r = lax.broadcasted_iota(jnp.int32, (t, t), 0)
                    c = lax.broadcasted_iota(jnp.int32, (t, t), 1)
                    s = jnp.where(c <= r, s, NEG_INF)
                _softmax_step(s, v_aug[h], m_ref, acc_ref, head)

    chunk(i, True)

    def body(j, carry):
        chunk(j, False)
        return carry

    lax.fori_loop(0, i, body, 0)
    for p in range(n_heads // 2):
        write(p, _finish_pair(acc_ref[2 * p], acc_ref[2 * p + 1]))


def _fox_keys_and_queries(i, q_ref, k_ref, c_ref, kaug_ref):
    t = ATT_TILE
    s_len = k_ref.shape[1]
    lane = _lane_iota((t, LANES))

    @pl.when(i == 0)
    def _():
        def build(r, carry):
            rows = pl.ds(pl.multiple_of(r * t, t), t)
            c = c_ref[0, rows, :]
            for p in range(N_FOX // 2):
                k2 = k_ref[0, rows, p * LANES:(p + 1) * LANES].astype(f32)
                for h in range(2):
                    head = 2 * p + h
                    base = _spare_base(h)
                    fl = F_LANE0 + head
                    pieces = _split3(-jnp.broadcast_to(c[:, fl:fl + 1], (t, LANES)))
                    ka = jnp.where(_own_lanes(lane, h), k2, 0.0)
                    for n, piece in enumerate(pieces):
                        ka = jnp.where(lane == base + n, piece, ka)
                    kaug_ref[head, rows, :] = ka.astype(bf16)
            return carry
        lax.fori_loop(0, s_len // t, build, 0)

    q_heads = []
    for p in range(N_FOX // 2):
        q2 = q_ref[0, :, p * LANES:(p + 1) * LANES].astype(f32)
        for h in range(2):
            base = _spare_base(h)
            ones = jnp.where((lane >= base) & (lane < base + N_SPLIT), 1.0, 0.0)
            q_heads.append(jnp.where(_own_lanes(lane, h), q2, ones).astype(bf16))
    return q_heads


def _moba_keys_and_queries(i, qc_ref, q_ref, k_ref, kaug_ref, kmean_ref, head0):
    t = ATT_TILE
    bs = MOBA_BLOCK
    s_len = k_ref.shape[1]
    n_blk = s_len // bs
    per_tile = t // bs
    lane = _lane_iota((t, LANES))
    mask_lane0 = 2 * N_SPLIT

    @pl.when(i == 0)
    def _():
        kmean_ref[...] = jnp.zeros(kmean_ref.shape, f32)

        def build(r, carry):
            rows = pl.ds(pl.multiple_of(r * t, t), t)
            pos = r * t + lax.broadcasted_iota(jnp.int32, (t, LANES), 0)
            blk = pos // bs
            in_blk = (pos - blk * bs).astype(f32)
            blk_start = (blk * bs).astype(f32)
            for p in range(N_MOBA // 2):
                k2 = k_ref[0, rows, p * LANES:(p + 1) * LANES].astype(f32)
                for u in range(per_tile):
                    kmean_ref[p, pl.ds(r * per_tile + u, 1), :] = jnp.mean(
                        k2[u * bs:(u + 1) * bs], axis=0, keepdims=True)
                for h in range(2):
                    base = _spare_base(h)
                    rel = lane - base
                    ka = jnp.where(_own_lanes(lane, h), k2, 0.0)
                    ka = jnp.where((rel >= 0) & (rel < N_SPLIT), in_blk, ka)
                    ka = jnp.where((rel >= N_SPLIT) & (rel < mask_lane0), blk_start, ka)
                    ka = jnp.where(rel - mask_lane0 == blk, MASK_VAL, ka)
                    kaug_ref[head0 + 2 * p + h, rows, :] = ka.astype(bf16)
            return carry
        lax.fori_loop(0, s_len // t, build, 0)

    sub = lax.broadcasted_iota(jnp.int32, (n_blk, t), 0)
    q_blk = i * per_tile + lax.broadcasted_iota(jnp.int32, (1, t), 1) // bs
    q_heads = []
    for p in range(N_MOBA // 2):
        q2 = q_ref[0, :, p * LANES:(p + 1) * LANES].astype(f32)
        for h in range(2):
            head = 2 * p + h
            q_own = jnp.where(_own_lanes(lane, h), q2, 0.0)
            means = jnp.concatenate(_split3(kmean_ref[p, 0:n_blk, :]), axis=0).astype(bf16)
            gate3 = _dot_nt(means, q_own.astype(bf16))
            gate = gate3[0:n_blk] + gate3[n_blk:2 * n_blk] + gate3[2 * n_blk:]
            rank = jnp.zeros((n_blk, t), f32)
            for j in range(n_blk):
                gj = gate[j:j + 1, :]
                beats = (gj > gate) | ((gj == gate) & (j < sub))
                rank = rank + jnp.where(beats & (j < q_blk), 1.0, 0.0)
            dropped = jnp.where((sub < q_blk) & (rank >= MOBA_TOPK), 1.0, 0.0)
            dropped = jnp.concatenate([dropped, jnp.zeros((LANES - n_blk, t), f32)], axis=0)
            dropped = pltpu.roll(dropped.T, _spare_base(h) + mask_lane0, 1)
            q_heads.append((q_own + qc_ref[head:head + 1, :] + dropped).astype(bf16))
    return q_heads


def _fox_moba_kernel(qc_ref, qf_ref, kf_ref, vf_ref, c_ref, qm_ref, km_ref, vm_ref,
                     of_ref, om_ref, kaug_ref, kmean_ref, m_ref, acc_ref):
    i = pl.program_id(1)
    q_heads = _fox_keys_and_queries(i, qf_ref, kf_ref, c_ref, kaug_ref)
    q_heads += _moba_keys_and_queries(i, qc_ref, qm_ref, km_ref, kaug_ref, kmean_ref, N_FOX)
    fox_pairs = N_FOX // 2

    def values(rows, p):
        if p < fox_pairs:
            return vf_ref[0, rows, p * LANES:(p + 1) * LANES]
        return vm_ref[0, rows, (p - fox_pairs) * LANES:(p - fox_pairs + 1) * LANES]

    def write(p, o):
        if p < fox_pairs:
            of_ref[0, :, p * LANES:(p + 1) * LANES] = o
        else:
            om_ref[0, :, (p - fox_pairs) * LANES:(p - fox_pairs + 1) * LANES] = o

    _causal_chunk_then_past(
        i, N_FOX + N_MOBA,
        lambda head, q_rows, k_rows: _dot_nt(q_heads[head][q_rows], kaug_ref[head, k_rows, :]),
        values, write, m_ref, acc_ref)


def _fox_moba_attention(moba_const, q, k, v, cum):
    b, s, _ = q.shape
    t = ATT_TILE
    fox_blk, moba_blk = OFF_FOX // W_FOX, OFF_MOBA // W_MOBA
    n_heads = N_FOX + N_MOBA
    assert 2 * N_SPLIT + s // MOBA_BLOCK <= HEAD_DIM and s // MOBA_BLOCK <= LANES
    tile = lambda width, blk: pl.BlockSpec((1, t, width), lambda bi, i: (bi, i, blk))
    whole = lambda width, blk: pl.BlockSpec((1, s, width), lambda bi, i: (bi, 0, blk))
    return pl.pallas_call(
        _fox_moba_kernel,
        grid=(b, s // t),
        in_specs=[
            pl.BlockSpec((8, LANES), lambda bi, i: (0, 0)),
            tile(W_FOX, fox_blk), whole(W_FOX, fox_blk), whole(W_FOX, fox_blk),
            whole(LANES, 0),
            tile(W_MOBA, moba_blk), whole(W_MOBA, moba_blk), whole(W_MOBA, moba_blk),
        ],
        out_specs=[tile(W_FOX, 0), tile(W_MOBA, 0)],
        out_shape=[jax.ShapeDtypeStruct((b, s, W_FOX), f32),
                   jax.ShapeDtypeStruct((b, s, W_MOBA), f32)],
        scratch_shapes=[
            pltpu.VMEM((n_heads, s, LANES), bf16),
            pltpu.VMEM((N_MOBA // 2, LANES, LANES), f32),
            pltpu.VMEM((n_heads, t, LANES), f32),
            pltpu.VMEM((n_heads, t, LANES), f32),
        ],
        compiler_params=_cparams(2),
        name="fox_moba_attention",
    )(moba_const, q, k, v, cum, q, k, v)


def _alibi_query_lanes(slopes):
    n = slopes.shape[0]
    pieces = _split3(slopes * LOG2E)
    lanes = jnp.arange(LANES)
    rows = []
    for head in range(n):
        rel = lanes - _spare_base(head % 2)
        row = jnp.zeros((LANES,), f32)
        for j in range(2 * N_SPLIT):
            row = jnp.where(rel == j, pieces[j % N_SPLIT][head], row)
        rows.append(row)
    return jnp.stack(rows)


def _dilated_kernel(qc_ref, *refs):
    n_br = len(DIL_PAIRS)
    qkv_refs = [refs[3 * g:3 * g + 3] for g in range(n_br)]
    o_ref, kpos_ref, m_ref, acc_ref = refs[3 * n_br:]
    s_len = o_ref.shape[1]
    blk = DIL_BLOCK
    n_blk = s_len // blk
    lane = _lane_iota((blk, LANES))
    lane_k = _lane_iota((2 * blk, LANES))

    @pl.when((pl.program_id(0) == 0) & (pl.program_id(1) == 0))
    def _():
        ch = 2 * blk
        lane_c = _lane_iota((ch, LANES))
        for g, (_, d) in enumerate(DIL_PAIRS):
            per_res = s_len // d // ch

            def fill(u, carry, g=g, d=d, per_res=per_res):
                res = u // per_res
                sub0 = (u - res * per_res) * ch
                pos = res + d * (sub0 + lax.broadcasted_iota(jnp.int32, (ch, LANES), 0))
                pos_lo = jnp.bitwise_and(pos, blk - 1)
                pos_hi = (pos - pos_lo).astype(f32)
                pos_lo = pos_lo.astype(f32)
                for h in range(2):
                    rel = lane_c - _spare_base(h)
                    val = jnp.where((rel >= 0) & (rel < N_SPLIT), pos_lo,
                                    jnp.where((rel >= N_SPLIT) & (rel < 2 * N_SPLIT), pos_hi, 0.0))
                    kpos_ref[2 * g + h, pl.ds(pl.multiple_of(u * ch, ch), ch), :] = val.astype(bf16)
                return carry

            lax.fori_loop(0, s_len // ch, fill, 0)

    q_const = [jnp.broadcast_to(qc_ref[0, h:h + 1, :], (blk, LANES)).astype(bf16) for h in range(2)]
    iq = lax.broadcasted_iota(jnp.int32, (blk, 2 * blk), 0) + blk
    ik = lax.broadcasted_iota(jnp.int32, (blk, 2 * blk), 1)
    off = iq - ik
    order = sorted(range(n_br), key=lambda g: -DIL_PAIRS[g][1])
    for step, g in enumerate(order):
        window, d = DIL_PAIRS[g]
        q_ref, k_ref, v_ref = qkv_refs[g]
        per_res = n_blk // d
        in_band = (off >= 0) & (off <= window // d)
        first, last = step == 0, step == n_br - 1
        assert not last or d == 1

        unroll = DIL_UNROLL // 2 if last else DIL_UNROLL

        def blocks(it, carry, g=g, d=d, per_res=per_res, in_band=in_band, first=first, last=last,
                   q_ref=q_ref, k_ref=k_ref, v_ref=v_ref, unroll=unroll):
            for u in range(unroll):
                nb = it * unroll + u
                res = nb // per_res
                n = nb - res * per_res
                own = pl.ds(pl.multiple_of(n * blk, blk), blk)
                prev = pl.ds(pl.multiple_of(jnp.maximum(n - 1, 0) * blk, blk), blk)
                flat_own = pl.ds(pl.multiple_of(nb * blk, blk), blk)
                flat_prev = pl.ds(pl.multiple_of(jnp.maximum(nb - 1, 0) * blk, blk), blk)
                q2 = q_ref[0, res, own, :]
                k2 = jnp.concatenate([k_ref[0, res, prev, :], k_ref[0, res, own, :]], axis=0)
                v2 = jnp.concatenate([v_ref[0, res, prev, :], v_ref[0, res, own, :]], axis=0)
                v_aug = _ones_on_other_head(v2)
                first_key = jnp.where(n > 0, 0, blk)
                allowed = in_band & (ik >= first_key)
                if d > 1:
                    nat = pl.ds(res + n * (blk * d), blk, stride=d)
                else:
                    nat = flat_own
                outs = []
                for h in range(2):
                    kp = jnp.concatenate([kpos_ref[2 * g + h, flat_prev, :],
                                          kpos_ref[2 * g + h, flat_own, :]], axis=0)
                    ka = jnp.where(_own_lanes(lane_k, h), k2, kp)
                    qa = jnp.where(_own_lanes(lane, h), q2, q_const[h])
                    s = jnp.where(allowed, _dot_nt(qa, ka), NEG_INF)
                    m_new = jnp.broadcast_to(jnp.max(s, axis=-1, keepdims=True), (blk, LANES))
                    if not first:
                        m_prev = m_ref[h, nat, :]
                        m_new = jnp.maximum(m_prev, m_new)
                    pr = jnp.exp2(s - jnp.concatenate([m_new, m_new], axis=1)).astype(bf16)
                    acc = jnp.dot(pr, v_aug[h], preferred_element_type=f32)
                    if not first:
                        acc = jnp.exp2(m_prev - m_new) * acc_ref[h, nat, :] + acc
                    if last:
                        outs.append(acc)
                    else:
                        m_ref[h, nat, :] = m_new
                        acc_ref[h, nat, :] = acc
                if last:
                    o_ref[0, nat, :] = _finish_pair(outs[0], outs[1])
            return carry

        lax.fori_loop(0, n_blk // unroll, blocks, 0)


def _dilated_attention(q_const, branch_qkv):
    b, _, s, _ = branch_qkv[0][0][0].shape
    n_br = len(DIL_PAIRS)
    assert s % (2 * DIL_BLOCK * DIL_PAIRS[-1][1]) == 0 and (s // DIL_BLOCK) % DIL_UNROLL == 0
    in_specs = [pl.BlockSpec((1, 8, LANES), lambda bi, p: (p, 0, 0))]
    args = [q_const]
    for (qkv, base), (_, d) in zip(branch_qkv, DIL_PAIRS):
        spec = pl.BlockSpec((1, d, s // d, LANES), lambda bi, p, base=base: (bi, 0, 0, base + p))
        in_specs += [spec] * 3
        args += list(qkv)
    return pl.pallas_call(
        _dilated_kernel,
        grid=(b, W_DIL // LANES),
        in_specs=in_specs,
        out_specs=pl.BlockSpec((1, s, LANES), lambda bi, p: (bi, 0, p)),
        out_shape=jax.ShapeDtypeStruct((b, s, W_DIL), f32),
        scratch_shapes=[
            pltpu.VMEM((2 * n_br, s, LANES), bf16),
            pltpu.VMEM((2, s, LANES), f32),
            pltpu.VMEM((2, s, LANES), f32),
        ],
        compiler_params=_cparams(2),
        name="dilated_attention",
    )(*args)


def _dilated_query_lanes(slopes):
    rows = _alibi_query_lanes(slopes)
    pairs = rows.reshape(N_DIL // 2, 2, LANES)
    return jnp.concatenate([pairs, jnp.zeros((N_DIL // 2, 6, LANES), f32)], axis=1)


def _rms(o, gain):
    return o * lax.rsqrt(jnp.mean(o * o, axis=-1, keepdims=True) + EPS) * gain


def _post_kernel(x_ref, of_ref, od_ref, om_ref, gf_ref, gain_ref, wout_ref,
                 g2_ref, wup_ref, wdown_ref, out_ref, y_ref, acc_ref):
    a0, a1 = OFF_DIL, OFF_MOBA
    y_fox = _rms(of_ref[...], gain_ref[:, 0:a0]) * jax.nn.sigmoid(gf_ref[...])
    y_ref[:, 0:a0] = y_fox.astype(bf16)
    y_ref[:, a0:a1] = _rms(od_ref[...], gain_ref[:, a0:a1]).astype(bf16)
    y_ref[:, a1:] = _rms(om_ref[...], gain_ref[:, a1:]).astype(bf16)
    x1 = x_ref[...] + jnp.dot(y_ref[...], wout_ref[...], preferred_element_type=f32)
    hn = _rms(x1, g2_ref[...]).astype(bf16)
    acc_ref[...] = x1
    for c in range(D_FF // QK_CHUNK):
        cols = slice(c * QK_CHUNK, (c + 1) * QK_CHUNK)
        h = jnp.maximum(jnp.dot(hn, wup_ref[:, cols], preferred_element_type=f32), 0.0)
        acc_ref[...] += jnp.dot((h * h).astype(bf16), wdown_ref[cols, :],
                                preferred_element_type=f32)
    out_ref[...] = acc_ref[...]


def _outproj_mlp(x2d, o_fox, o_dil, o_moba, aux, gain, w_out, g2, w_up, w_down):
    m = x2d.shape[0]
    row = lambda i: (i, 0)
    const = lambda i: (0, 0)
    resident = lambda shape: pl.BlockSpec(shape, const, pipeline_mode=pl.Buffered(1))
    return pl.pallas_call(
        _post_kernel,
        grid=(m // ROW_TILE,),
        in_specs=[
            pl.BlockSpec((ROW_TILE, D_MODEL), row),
            pl.BlockSpec((ROW_TILE, W_FOX), row),
            pl.BlockSpec((ROW_TILE, W_DIL), row),
            pl.BlockSpec((ROW_TILE, W_MOBA), row),
            pl.BlockSpec((ROW_TILE, W_FOX), row),
            pl.BlockSpec((1, D_MODEL), const),
            resident((D_MODEL, D_MODEL)),
            pl.BlockSpec((1, D_MODEL), const),
            resident((D_MODEL, D_FF)),
            resident((D_FF, D_MODEL)),
        ],
        out_specs=pl.BlockSpec((ROW_TILE, D_MODEL), row),
        out_shape=jax.ShapeDtypeStruct((m, D_MODEL), f32),
        scratch_shapes=[pltpu.VMEM((ROW_TILE, D_MODEL), bf16), pltpu.VMEM((ROW_TILE, D_MODEL), f32)],
        compiler_params=_cparams(1),
        name="outproj_mlp",
    )(x2d, o_fox, o_dil, o_moba, aux, gain, w_out, g2, w_up, w_down)


def _alibi_slopes():
    return jnp.exp2(-8.0 * jnp.arange(1, N_ALIBI + 1, dtype=f32) / N_ALIBI)


def _head_mean_matrix():
    r = jnp.arange(QK_CHUNK)[:, None] // HEAD_DIM
    c = jnp.arange(QK_CHUNK)[None, :] // HEAD_DIM
    return jnp.where(r == c, 1.0 / HEAD_DIM, 0.0).astype(bf16)


def _permute_families(a, axis):
    moba, fox, dil = jnp.split(a, [W_MOBA, W_MOBA + W_FOX], axis=axis)
    return jnp.concatenate([fox, dil, moba], axis=axis)


def kernel(x, attn_norm, w_in, b_forget, q_gain, k_gain, out_gain, w_out, mlp_norm, w_up, w_down):
    b, s, d = x.shape
    assert d == D_MODEL and s % ATT_TILE == 0 and (b * s) % ROW_TILE == 0
    depth = w_in.shape[0]
    slopes = _alibi_slopes()
    moba_lanes = jnp.concatenate(
        [_alibi_query_lanes(slopes[N_DIL:]), jnp.zeros((8 - N_MOBA, LANES), f32)])
    dil_lanes = _dilated_query_lanes(slopes[:N_DIL])
    gmat = _head_mean_matrix()
    x2d = x.reshape(b * s, d)
    for l in range(depth):
        wqkv, waux = _inproj_weights(w_in, l)
        baux = jnp.concatenate(
            [jnp.zeros((AUX_COLS - N_FOX,), f32), b_forget[l]]).reshape(1, AUX_COLS)
        gain = jnp.concatenate(
            [_permute_families(q_gain[l].reshape(1, d), 1) * (ATTN_SCALE * LOG2E),
             _permute_families(k_gain[l].reshape(1, d), 1)], axis=1)
        q, k, v, aux, *gathered = _inproj(x2d, attn_norm[l].reshape(1, d), wqkv, waux, baux, gain,
                                          gmat, s)
        q3, k3, v3 = (a.reshape(b, s, d) for a in (q, k, v))
        branches = [((q3.reshape(b, 1, s, d), k3.reshape(b, 1, s, d), v3.reshape(b, 1, s, d)),
                     OFF_DIL // LANES)]
        for j in range(len(DIL_GATHER)):
            branches.append((tuple(gathered[t * len(DIL_GATHER) + j] for t in range(3)), 0))
        cum = _fox_scan(aux.reshape(b, s, AUX_COLS))
        o_fox, o_moba = _fox_moba_attention(moba_lanes, q3, k3, v3, cum)
        o_dil = _dilated_attention(dil_lanes, branches)
        x2d = _outproj_mlp(x2d, o_fox.reshape(b * s, W_FOX), o_dil.reshape(b * s, W_DIL),
                           o_moba.reshape(b * s, W_MOBA), aux,
                           _permute_families(out_gain[l].reshape(1, d), 1),
                           _permute_families(w_out[l], 0).astype(bf16),
                           mlp_norm[l].reshape(1, d), w_up[l].astype(bf16), w_down[l].astype(bf16))
    return x2d.reshape(b, s, d)
```

```python
import functools
import math

import jax
import jax.numpy as jnp
from jax import lax
from jax.experimental import pallas as pl
from jax.experimental.pallas import tpu as pltpu

D_MODEL = 1024
HEAD_DIM = 64
N_HEADS = D_MODEL // HEAD_DIM
N_MOBA = N_HEADS // 4
N_FOX = (N_HEADS - N_MOBA) // 2
N_DIL = N_HEADS - N_MOBA - N_FOX
N_ALIBI = N_MOBA + N_DIL
MOBA_BLOCK = 256
MOBA_TOPK = 3
DIL_PAIRS = ((128, 1), (512, 4), (2048, 16))
DIL_BLOCK = 128
D_FF = 4 * D_MODEL
QKV_COLS = 3 * D_MODEL
W_MOBA = N_MOBA * HEAD_DIM
W_FOX = N_FOX * HEAD_DIM
W_DIL = N_DIL * HEAD_DIM
ATTN_SCALE = HEAD_DIM ** -0.5
EPS = 1e-6
NEG_INF = -1e30
LOG2E = math.log2(math.e)

LANES = 128
AUX_COLS = 512
F_COL_BLOCK = AUX_COLS // LANES - 1
F_LANE0 = LANES - N_FOX
VMEM_LIMIT = 56 * 1024 * 1024

ROW_TILE = 512
QK_CHUNK = 512
ATT_TILE = 512
SCAN_TILE = 128
SCAN_BATCH = 4
DIL_UNROLL = 16

OFF_FOX, OFF_DIL, OFF_MOBA = 0, W_FOX, W_FOX + W_DIL
MASK_VAL = -(2.0 ** 100)
N_SPLIT = 3

f32 = jnp.float32
bf16 = jnp.bfloat16


def _cparams(n_axes):
    return pltpu.CompilerParams(
        dimension_semantics=("arbitrary",) * n_axes,
        vmem_limit_bytes=VMEM_LIMIT,
    )


def _dot_nt(a, b):
    return lax.dot_general(a, b, (((1,), (1,)), ((), ())), preferred_element_type=f32)


def _lane_iota(shape):
    return lax.broadcasted_iota(jnp.int32, shape, len(shape) - 1)


def _split3(x):
    hi = x.astype(bf16).astype(f32)
    r = x - hi
    mid = r.astype(bf16).astype(f32)
    lo = (r - mid).astype(bf16).astype(f32)
    return hi, mid, lo


def _own_lanes(lane, h):
    return (lane < HEAD_DIM) if h == 0 else (lane >= HEAD_DIM)


def _spare_base(h):
    return HEAD_DIM if h == 0 else 0


def _cast_kernel(w_ref, o_ref):
    o_ref[...] = w_ref[0].astype(bf16)


def _cast_aux_kernel(w_ref, o_ref):
    col = _lane_iota((D_MODEL, AUX_COLS))
    w = jnp.where(col < N_FOX + W_FOX, w_ref[0], 0.0)
    o_ref[...] = pltpu.roll(w, AUX_COLS - N_FOX, 1).astype(bf16)


def _inproj_weights(w_in, l):
    d = D_MODEL
    n_lane_blocks = d // LANES
    moba_blocks = W_MOBA // LANES

    def src_block(j):
        return (j // n_lane_blocks) * n_lane_blocks + (j % n_lane_blocks + moba_blocks) % n_lane_blocks

    wqkv = pl.pallas_call(
        _cast_kernel,
        grid=(QKV_COLS // LANES,),
        in_specs=[pl.BlockSpec((1, d, LANES), lambda j: (l, 0, src_block(j)))],
        out_specs=pl.BlockSpec((d, LANES), lambda j: (0, j)),
        out_shape=jax.ShapeDtypeStruct((d, QKV_COLS), bf16),
        compiler_params=_cparams(1),
        name="cast_qkv",
    )(w_in)
    waux = pl.pallas_call(
        _cast_aux_kernel,
        grid=(1,),
        in_specs=[pl.BlockSpec((1, d, AUX_COLS), lambda j: (l, 0, QKV_COLS // AUX_COLS))],
        out_specs=pl.BlockSpec((d, AUX_COLS), lambda j: (0, 0)),
        out_shape=jax.ShapeDtypeStruct((d, AUX_COLS), bf16),
        compiler_params=_cparams(1),
        name="cast_aux",
    )(w_in)
    return wqkv, waux


DIL_GATHER = tuple(d for _, d in DIL_PAIRS if d > 1)


def _inproj_kernel(x_ref, g_ref, wqkv_ref, waux_ref, baux_ref, gain_ref, gmat_ref,
                   q_ref, k_ref, v_ref, aux_ref, *rest):
    res_refs, stage_ref, stage2_ref = rest[:-2], rest[-2], rest[-1]
    x = x_ref[...]
    ms = jnp.mean(x * x, axis=-1, keepdims=True)
    hn = (x * lax.rsqrt(ms + EPS) * g_ref[...]).astype(bf16)
    n_chunks = QKV_COLS // QK_CHUNK
    per = D_MODEL // QK_CHUNK
    for c in range(n_chunks):
        cols = slice(c * QK_CHUNK, (c + 1) * QK_CHUNK)
        acc = jnp.dot(hn, wqkv_ref[:, cols], preferred_element_type=f32)
        tensor = c // per
        dst = (q_ref, k_ref, v_ref)[tensor]
        chunk0 = (c % per) * QK_CHUNK
        if tensor < 2:
            msq = jnp.dot((acc * acc).astype(bf16), gmat_ref[...], preferred_element_type=f32)
            acc = acc * lax.rsqrt(msq + EPS) * gain_ref[:, cols]
        dst[:, chunk0:chunk0 + QK_CHUNK] = acc.astype(bf16)
        for slab in range(W_DIL // LANES):
            col0 = OFF_DIL + slab * LANES - chunk0
            if 0 <= col0 < QK_CHUNK:
                stage_ref[tensor, slab] = acc[:, col0:col0 + LANES]
    aux_ref[...] = jnp.dot(hn, waux_ref[...], preferred_element_type=f32) + baux_ref[...]
    d1, d2 = DIL_GATHER
    assert d2 == d1 * d1
    for tensor in range(3):
        out1, out2 = res_refs[tensor * 2], res_refs[tensor * 2 + 1]
        for slab in range(W_DIL // LANES):
            cols = slice(slab * LANES, (slab + 1) * LANES)
            for r in range(d1):
                rows = stage_ref[tensor, slab, pl.ds(r, ROW_TILE // d1, stride=d1), :]
                out1[0, r, :, cols] = rows.astype(bf16)
                stage2_ref[tensor, slab, r] = rows
            for r in range(d1):
                for w in range(d1):
                    rows = stage2_ref[tensor, slab, r, pl.ds(w, ROW_TILE // d2, stride=d1), :]
                    out2[0, r + d1 * w, :, cols] = rows.astype(bf16)


def _inproj(x2d, g, wqkv, waux, baux, gain, gmat, seq):
    m = x2d.shape[0]
    tiles_per_seq = seq // ROW_TILE
    row = lambda i: (i, 0)
    const = lambda i: (0, 0)
    res_specs, res_shapes = [], []
    for _ in range(3):
        for d in DIL_GATHER:
            res_specs.append(pl.BlockSpec((1, d, ROW_TILE // d, W_DIL),
                                          lambda i: (i // tiles_per_seq, 0, i % tiles_per_seq, 0)))
            res_shapes.append(jax.ShapeDtypeStruct((m // seq, d, seq // d, W_DIL), bf16))
    return pl.pallas_call(
        _inproj_kernel,
        grid=(m // ROW_TILE,),
        in_specs=[
            pl.BlockSpec((ROW_TILE, D_MODEL), row),
            pl.BlockSpec((1, D_MODEL), const),
            pl.BlockSpec((D_MODEL, QKV_COLS), const),
            pl.BlockSpec((D_MODEL, AUX_COLS), const),
            pl.BlockSpec((1, AUX_COLS), const),
            pl.BlockSpec((1, 2 * D_MODEL), const),
            pl.BlockSpec((QK_CHUNK, QK_CHUNK), const),
        ],
        out_specs=[
            pl.BlockSpec((ROW_TILE, D_MODEL), row),
            pl.BlockSpec((ROW_TILE, D_MODEL), row),
            pl.BlockSpec((ROW_TILE, D_MODEL), row),
            pl.BlockSpec((ROW_TILE, AUX_COLS), row),
        ] + res_specs,
        out_shape=[
            jax.ShapeDtypeStruct((m, D_MODEL), bf16),
            jax.ShapeDtypeStruct((m, D_MODEL), bf16),
            jax.ShapeDtypeStruct((m, D_MODEL), bf16),
            jax.ShapeDtypeStruct((m, AUX_COLS), f32),
        ] + res_shapes,
        scratch_shapes=[
            pltpu.VMEM((3, W_DIL // LANES, ROW_TILE, LANES), f32),
            pltpu.VMEM((3, W_DIL // LANES, DIL_GATHER[0], ROW_TILE // DIL_GATHER[0], LANES), f32),
        ],
        compiler_params=_cparams(1),
        name="inproj",
    )(x2d, g, wqkv, waux, baux, gain, gmat)


def _scan_kernel(f_ref, c_ref):
    n_batch = f_ref.shape[0]
    n = f_ref.shape[1] // SCAN_TILE
    r = lax.broadcasted_iota(jnp.int32, (SCAN_TILE, SCAN_TILE), 0)
    c = lax.broadcasted_iota(jnp.int32, (SCAN_TILE, SCAN_TILE), 1)
    tri = jnp.where(r >= c, 1.0, 0.0).astype(bf16)

    def body(i, carry):
        rows = pl.ds(pl.multiple_of(i * SCAN_TILE, SCAN_TILE), SCAN_TILE)
        last = []
        for bi in range(n_batch):
            f = f_ref[bi, rows, :]
            ls = jnp.minimum(f, 0.0) - jnp.log(1.0 + jnp.exp(-jnp.abs(f)))
            pieces = jnp.concatenate(_split3(ls), axis=1).astype(bf16)
            part = jnp.dot(tri, pieces, preferred_element_type=f32)
            cum = (part[:, :LANES] + part[:, LANES:2 * LANES] + part[:, 2 * LANES:]) + carry[bi]
            c_ref[bi, rows, :] = cum * LOG2E
            last.append(cum[SCAN_TILE - 1:SCAN_TILE, :])
        return tuple(last)

    lax.fori_loop(0, n, body, tuple(jnp.zeros((1, LANES), f32) for _ in range(n_batch)))


def _fox_scan(aux3):
    b, s, _ = aux3.shape
    per_step = math.gcd(b, SCAN_BATCH)
    return pl.pallas_call(
        _scan_kernel,
        grid=(b // per_step,),
        in_specs=[pl.BlockSpec((per_step, s, LANES), lambda i: (i, 0, F_COL_BLOCK))],
        out_specs=pl.BlockSpec((per_step, s, LANES), lambda i: (i, 0, 0)),
        out_shape=jax.ShapeDtypeStruct((b, s, LANES), f32),
        compiler_params=_cparams(1),
        name="fox_scan",
    )(aux3)


def _softmax_step(s, v_aug, m_ref, acc_ref, head, rows=slice(None)):
    m_prev = m_ref[head, rows, :]
    row_max = jnp.max(s, axis=-1, keepdims=True)
    m_new = jnp.maximum(m_prev, jnp.broadcast_to(row_max, m_prev.shape))
    alpha = jnp.exp2(m_prev - m_new)
    m_wide = jnp.concatenate([m_new] * (s.shape[1] // LANES), axis=1)
    p = jnp.exp2((s - m_wide).astype(bf16))
    acc_ref[head, rows, :] = (alpha * acc_ref[head, rows, :]
                              + jnp.dot(p, v_aug, preferred_element_type=f32))
    m_ref[head, rows, :] = m_new


def _ones_on_other_head(v2):
    first = _lane_iota(v2.shape) < HEAD_DIM
    one = jnp.ones_like(v2)
    return jnp.where(first, v2, one), jnp.where(first, one, v2)


def _finish_pair(acc_a, acc_b):
    o_a = acc_a / pltpu.roll(acc_a, HEAD_DIM, 1)
    o_b = acc_b / pltpu.roll(acc_b, HEAD_DIM, 1)
    return jnp.where(_lane_iota(acc_a.shape) < HEAD_DIM, o_a, o_b)


def _causal_chunk_then_past(i, n_heads, scores, values, write, m_ref, acc_ref):
    t = ATT_TILE
    m_ref[...] = jnp.full(m_ref.shape, NEG_INF, f32)
    acc_ref[...] = jnp.zeros(acc_ref.shape, f32)

    def chunk(j, causal):
        rows = pl.ds(pl.multiple_of(j * t, t), t)
        for p in range(n_heads // 2):
            v_aug = _ones_on_other_head(values(rows, p))
            for h in range(2):
                head = 2 * p + h
                s = scores(head, slice(None), rows)
                if causal:
                    r = lax.broadcasted_iota(jnp.int32, (t, t), 0)
                    c = lax.broadcasted_iota(jnp.int32, (t, t), 1)
                    s = jnp.where(c <= r, s, NEG_INF)
                _softmax_step(s, v_aug[h], m_ref, acc_ref, head)

    chunk(i, True)

    def body(j, carry):
        chunk(j, False)
        return carry

    lax.fori_loop(0, i, body, 0)
    for p in range(n_heads // 2):
        write(p, _finish_pair(acc_ref[2 * p], acc_ref[2 * p + 1]))


def _fox_keys_and_queries(i, q_ref, k_ref, c_ref, kaug_ref):
    t = ATT_TILE
    s_len = k_ref.shape[1]
    lane = _lane_iota((t, LANES))

    @pl.when(i == 0)
    def _():
        def build(r, carry):
            rows = pl.ds(pl.multiple_of(r * t, t), t)
            c = c_ref[0, rows, :]
            for p in range(N_FOX // 2):
                k2 = k_ref[0, rows, p * LANES:(p + 1) * LANES].astype(f32)
                for h in range(2):
                    head = 2 * p + h
                    base = _spare_base(h)
                    fl = F_LANE0 + head
                    pieces = _split3(-jnp.broadcast_to(c[:, fl:fl + 1], (t, LANES)))
                    ka = jnp.where(_own_lanes(lane, h), k2, 0.0)
                    for n, piece in enumerate(pieces):
                        ka = jnp.where(lane == base + n, piece, ka)
                    kaug_ref[head, rows, :] = ka.astype(bf16)
            return carry
        lax.fori_loop(0, s_len // t, build, 0)

    q_heads = []
    for p in range(N_FOX // 2):
        q2 = q_ref[0, :, p * LANES:(p + 1) * LANES].astype(f32)
        for h in range(2):
            base = _spare_base(h)
            ones = jnp.where((lane >= base) & (lane < base + N_SPLIT), 1.0, 0.0)
            q_heads.append(jnp.where(_own_lanes(lane, h), q2, ones).astype(bf16))
    return q_heads


def _moba_keys_and_queries(i, qc_ref, q_ref, k_ref, kaug_ref, kmean_ref, head0):
    t = ATT_TILE
    bs = MOBA_BLOCK
    s_len = k_ref.shape[1]
    n_blk = s_len // bs
    per_tile = t // bs
    lane = _lane_iota((t, LANES))
    mask_lane0 = 2 * N_SPLIT

    @pl.when(i == 0)
    def _():
        kmean_ref[...] = jnp.zeros(kmean_ref.shape, f32)

        def build(r, carry):
            rows = pl.ds(pl.multiple_of(r * t, t), t)
            pos = r * t + lax.broadcasted_iota(jnp.int32, (t, LANES), 0)
            blk = pos // bs
            in_blk = (pos - blk * bs).astype(f32)
            blk_start = (blk * bs).astype(f32)
            for p in range(N_MOBA // 2):
                k2 = k_ref[0, rows, p * LANES:(p + 1) * LANES].astype(f32)
                for u in range(per_tile):
                    kmean_ref[p, pl.ds(r * per_tile + u, 1), :] = jnp.mean(
                        k2[u * bs:(u + 1) * bs], axis=0, keepdims=True)
                for h in range(2):
                    base = _spare_base(h)
                    rel = lane - base
                    ka = jnp.where(_own_lanes(lane, h), k2, 0.0)
                    ka = jnp.where((rel >= 0) & (rel < N_SPLIT), in_blk, ka)
                    ka = jnp.where((rel >= N_SPLIT) & (rel < mask_lane0), blk_start, ka)
                    ka = jnp.where(rel - mask_lane0 == blk, MASK_VAL, ka)
                    kaug_ref[head0 + 2 * p + h, rows, :] = ka.astype(bf16)
            return carry
        lax.fori_loop(0, s_len // t, build, 0)

    sub = lax.broadcasted_iota(jnp.int32, (n_blk, t), 0)
    q_blk = i * per_tile + lax.broadcasted_iota(jnp.int32, (1, t), 1) // bs
    q_heads = []
    for p in range(N_MOBA // 2):
        q2 = q_ref[0, :, p * LANES:(p + 1) * LANES].astype(f32)
        for h in range(2):
            head = 2 * p + h
            q_own = jnp.where(_own_lanes(lane, h), q2, 0.0)
            means = jnp.concatenate(_split3(kmean_ref[p, 0:n_blk, :]), axis=0).astype(bf16)
            gate3 = _dot_nt(means, q_own.astype(bf16))
            gate = gate3[0:n_blk] + gate3[n_blk:2 * n_blk] + gate3[2 * n_blk:]
            rank = jnp.zeros((n_blk, t), f32)
            for j in range(n_blk):
                gj = gate[j:j + 1, :]
                beats = (gj > gate) | ((gj == gate) & (j < sub))
                rank = rank + jnp.where(beats & (j < q_blk), 1.0, 0.0)
            dropped = jnp.where((sub < q_blk) & (rank >= MOBA_TOPK), 1.0, 0.0)
            dropped = jnp.concatenate([dropped, jnp.zeros((LANES - n_blk, t), f32)], axis=0)
            dropped = pltpu.roll(dropped.T, _spare_base(h) + mask_lane0, 1)
            q_heads.append((q_own + qc_ref[head:head + 1, :] + dropped).astype(bf16))
    return q_heads


def _fox_moba_kernel(qc_ref, qf_ref, kf_ref, vf_ref, c_ref, qm_ref, km_ref, vm_ref,
                     of_ref, om_ref, kaug_ref, kmean_ref, m_ref, acc_ref):
    i = pl.program_id(1)
    q_heads = _fox_keys_and_queries(i, qf_ref, kf_ref, c_ref, kaug_ref)
    q_heads += _moba_keys_and_queries(i, qc_ref, qm_ref, km_ref, kaug_ref, kmean_ref, N_FOX)
    fox_pairs = N_FOX // 2

    def values(rows, p):
        if p < fox_pairs:
            return vf_ref[0, rows, p * LANES:(p + 1) * LANES]
        return vm_ref[0, rows, (p - fox_pairs) * LANES:(p - fox_pairs + 1) * LANES]

    def write(p, o):
        if p < fox_pairs:
            of_ref[0, :, p * LANES:(p + 1) * LANES] = o
        else:
            om_ref[0, :, (p - fox_pairs) * LANES:(p - fox_pairs + 1) * LANES] = o

    _causal_chunk_then_past(
        i, N_FOX + N_MOBA,
        lambda head, q_rows, k_rows: _dot_nt(q_heads[head][q_rows], kaug_ref[head, k_rows, :]),
        values, write, m_ref, acc_ref)


def _fox_moba_attention(moba_const, q, k, v, cum):
    b, s, _ = q.shape
    t = ATT_TILE
    fox_blk, moba_blk = OFF_FOX // W_FOX, OFF_MOBA // W_MOBA
    n_heads = N_FOX + N_MOBA
    assert 2 * N_SPLIT + s // MOBA_BLOCK <= HEAD_DIM and s // MOBA_BLOCK <= LANES
    tile = lambda width, blk: pl.BlockSpec((1, t, width), lambda bi, i: (bi, i, blk))
    whole = lambda width, blk: pl.BlockSpec((1, s, width), lambda bi, i: (bi, 0, blk))
    return pl.pallas_call(
        _fox_moba_kernel,
        grid=(b, s // t),
        in_specs=[
            pl.BlockSpec((8, LANES), lambda bi, i: (0, 0)),
            tile(W_FOX, fox_blk), whole(W_FOX, fox_blk), whole(W_FOX, fox_blk),
            whole(LANES, 0),
            tile(W_MOBA, moba_blk), whole(W_MOBA, moba_blk), whole(W_MOBA, moba_blk),
        ],
        out_specs=[tile(W_FOX, 0), tile(W_MOBA, 0)],
        out_shape=[jax.ShapeDtypeStruct((b, s, W_FOX), f32),
                   jax.ShapeDtypeStruct((b, s, W_MOBA), f32)],
        scratch_shapes=[
            pltpu.VMEM((n_heads, s, LANES), bf16),
            pltpu.VMEM((N_MOBA // 2, LANES, LANES), f32),
            pltpu.VMEM((n_heads, t, LANES), f32),
            pltpu.VMEM((n_heads, t, LANES), f32),
        ],
        compiler_params=_cparams(2),
        name="fox_moba_attention",
    )(moba_const, q, k, v, cum, q, k, v)


def _alibi_query_lanes(slopes):
    n = slopes.shape[0]
    pieces = _split3(slopes * LOG2E)
    lanes = jnp.arange(LANES)
    rows = []
    for head in range(n):
        rel = lanes - _spare_base(head % 2)
        row = jnp.zeros((LANES,), f32)
        for j in range(2 * N_SPLIT):
            row = jnp.where(rel == j, pieces[j % N_SPLIT][head], row)
        rows.append(row)
    return jnp.stack(rows)


def _dilated_kernel(qc_ref, *refs):
    n_br = len(DIL_PAIRS)
    qkv_refs = [refs[3 * g:3 * g + 3] for g in range(n_br)]
    o_ref, kpos_ref, m_ref, acc_ref = refs[3 * n_br:]
    s_len = o_ref.shape[1]
    blk = DIL_BLOCK
    n_blk = s_len // blk
    lane = _lane_iota((blk, LANES))
    lane_k = _lane_iota((2 * blk, LANES))

    @pl.when((pl.program_id(0) == 0) & (pl.program_id(1) == 0))
    def _():
        ch = 2 * blk
        lane_c = _lane_iota((ch, LANES))
        for g, (_, d) in enumerate(DIL_PAIRS):
            per_res = s_len // d // ch

            def fill(u, carry, g=g, d=d, per_res=per_res):
                res = u // per_res
                sub0 = (u - res * per_res) * ch
                pos = res + d * (sub0 + lax.broadcasted_iota(jnp.int32, (ch, LANES), 0))
                pos_lo = jnp.bitwise_and(pos, blk - 1)
                pos_hi = (pos - pos_lo).astype(f32)
                pos_lo = pos_lo.astype(f32)
                for h in range(2):
                    rel = lane_c - _spare_base(h)
                    val = jnp.where((rel >= 0) & (rel < N_SPLIT), pos_lo,
                                    jnp.where((rel >= N_SPLIT) & (rel < 2 * N_SPLIT), pos_hi, 0.0))
                    kpos_ref[2 * g + h, pl.ds(pl.multiple_of(u * ch, ch), ch), :] = val.astype(bf16)
                return carry

            lax.fori_loop(0, s_len // ch, fill, 0)

    q_const = [jnp.broadcast_to(qc_ref[0, h:h + 1, :], (blk, LANES)).astype(bf16) for h in range(2)]
    iq = lax.broadcasted_iota(jnp.int32, (blk, 2 * blk), 0) + blk
    ik = lax.broadcasted_iota(jnp.int32, (blk, 2 * blk), 1)
    off = iq - ik
    order = sorted(range(n_br), key=lambda g: -DIL_PAIRS[g][1])
    for step, g in enumerate(order):
        window, d = DIL_PAIRS[g]
        q_ref, k_ref, v_ref = qkv_refs[g]
        per_res = n_blk // d
        in_band = (off >= 0) & (off <= window // d)
        first, last = step == 0, step == n_br - 1
        assert not last or d == 1

        unroll = DIL_UNROLL // 2 if last else DIL_UNROLL

        def blocks(it, carry, g=g, d=d, per_res=per_res, in_band=in_band, first=first, last=last,
                   q_ref=q_ref, k_ref=k_ref, v_ref=v_ref, unroll=unroll):
            for u in range(unroll):
                nb = it * unroll + u
                res = nb // per_res
                n = nb - res * per_res
                own = pl.ds(pl.multiple_of(n * blk, blk), blk)
                prev = pl.ds(pl.multiple_of(jnp.maximum(n - 1, 0) * blk, blk), blk)
                flat_own = pl.ds(pl.multiple_of(nb * blk, blk), blk)
                flat_prev = pl.ds(pl.multiple_of(jnp.maximum(nb - 1, 0) * blk, blk), blk)
                q2 = q_ref[0, res, own, :]
                k2 = jnp.concatenate([k_ref[0, res, prev, :], k_ref[0, res, own, :]], axis=0)
                v2 = jnp.concatenate([v_ref[0, res, prev, :], v_ref[0, res, own, :]], axis=0)
                v_aug = _ones_on_other_head(v2)
                first_key = jnp.where(n > 0, 0, blk)
                allowed = in_band & (ik >= first_key)
                if d > 1:
                    nat = pl.ds(res + n * (blk * d), blk, stride=d)
                else:
                    nat = flat_own
                outs = []
                for h in range(2):
                    kp = jnp.concatenate([kpos_ref[2 * g + h, flat_prev, :],
                                          kpos_ref[2 * g + h, flat_own, :]], axis=0)
                    ka = jnp.where(_own_lanes(lane_k, h), k2, kp)
                    qa = jnp.where(_own_lanes(lane, h), q2, q_const[h])
                    s = jnp.where(allowed, _dot_nt(qa, ka), NEG_INF)
                    m_new = jnp.broadcast_to(jnp.max(s, axis=-1, keepdims=True), (blk, LANES))
                    if not first:
                        m_prev = m_ref[h, nat, :]
                        m_new = jnp.maximum(m_prev, m_new)
                    pr = jnp.exp2((s - jnp.concatenate([m_new, m_new], axis=1)).astype(bf16))
                    acc = jnp.dot(pr, v_aug[h], preferred_element_type=f32)
                    if not first:
                        acc = jnp.exp2(m_prev - m_new) * acc_ref[h, nat, :] + acc
                    if last:
                        outs.append(acc)
                    else:
                        m_ref[h, nat, :] = m_new
                        acc_ref[h, nat, :] = acc
                if last:
                    o_ref[0, nat, :] = _finish_pair(outs[0], outs[1])
            return carry

        lax.fori_loop(0, n_blk // unroll, blocks, 0)


def _dilated_attention(q_const, branch_qkv):
    b, _, s, _ = branch_qkv[0][0][0].shape
    n_br = len(DIL_PAIRS)
    assert s % (2 * DIL_BLOCK * DIL_PAIRS[-1][1]) == 0 and (s // DIL_BLOCK) % DIL_UNROLL == 0
    in_specs = [pl.BlockSpec((1, 8, LANES), lambda bi, p: (p, 0, 0))]
    args = [q_const]
    for (qkv, base), (_, d) in zip(branch_qkv, DIL_PAIRS):
        spec = pl.BlockSpec((1, d, s // d, LANES), lambda bi, p, base=base: (bi, 0, 0, base + p))
        in_specs += [spec] * 3
        args += list(qkv)
    return pl.pallas_call(
        _dilated_kernel,
        grid=(b, W_DIL // LANES),
        in_specs=in_specs,
        out_specs=pl.BlockSpec((1, s, LANES), lambda bi, p: (bi, 0, p)),
        out_shape=jax.ShapeDtypeStruct((b, s, W_DIL), f32),
        scratch_shapes=[
            pltpu.VMEM((2 * n_br, s, LANES), bf16),
            pltpu.VMEM((2, s, LANES), f32),
            pltpu.VMEM((2, s, LANES), f32),
        ],
        compiler_params=_cparams(2),
        name="dilated_attention",
    )(*args)


def _dilated_query_lanes(slopes):
    rows = _alibi_query_lanes(slopes)
    pairs = rows.reshape(N_DIL // 2, 2, LANES)
    return jnp.concatenate([pairs, jnp.zeros((N_DIL // 2, 6, LANES), f32)], axis=1)


def _rms(o, gain):
    return o * lax.rsqrt(jnp.mean(o * o, axis=-1, keepdims=True) + EPS) * gain


def _post_kernel(x_ref, of_ref, od_ref, om_ref, gf_ref, gain_ref, wout_ref,
                 g2_ref, wup_ref, wdown_ref, out_ref, y_ref, acc_ref):
    a0, a1 = OFF_DIL, OFF_MOBA
    y_fox = _rms(of_ref[...], gain_ref[:, 0:a0]) * jax.nn.sigmoid(gf_ref[...])
    y_ref[:, 0:a0] = y_fox.astype(bf16)
    y_ref[:, a0:a1] = _rms(od_ref[...], gain_ref[:, a0:a1]).astype(bf16)
    y_ref[:, a1:] = _rms(om_ref[...], gain_ref[:, a1:]).astype(bf16)
    x1 = x_ref[...] + jnp.dot(y_ref[...], wout_ref[...], preferred_element_type=f32)
    hn = _rms(x1, g2_ref[...]).astype(bf16)
    acc_ref[...] = x1
    for c in range(D_FF // QK_CHUNK):
        cols = slice(c * QK_CHUNK, (c + 1) * QK_CHUNK)
        h = jnp.maximum(jnp.dot(hn, wup_ref[:, cols], preferred_element_type=f32), 0.0)
        acc_ref[...] += jnp.dot((h * h).astype(bf16), wdown_ref[cols, :],
                                preferred_element_type=f32)
    out_ref[...] = acc_ref[...]


def _outproj_mlp(x2d, o_fox, o_dil, o_moba, aux, gain, w_out, g2, w_up, w_down):
    m = x2d.shape[0]
    row = lambda i: (i, 0)
    const = lambda i: (0, 0)
    resident = lambda shape: pl.BlockSpec(shape, const, pipeline_mode=pl.Buffered(1))
    return pl.pallas_call(
        _post_kernel,
        grid=(m // ROW_TILE,),
        in_specs=[
            pl.BlockSpec((ROW_TILE, D_MODEL), row),
            pl.BlockSpec((ROW_TILE, W_FOX), row),
            pl.BlockSpec((ROW_TILE, W_DIL), row),
            pl.BlockSpec((ROW_TILE, W_MOBA), row),
            pl.BlockSpec((ROW_TILE, W_FOX), row),
            pl.BlockSpec((1, D_MODEL), const),
            resident((D_MODEL, D_MODEL)),
            pl.BlockSpec((1, D_MODEL), const),
            resident((D_MODEL, D_FF)),
            resident((D_FF, D_MODEL)),
        ],
        out_specs=pl.BlockSpec((ROW_TILE, D_MODEL), row),
        out_shape=jax.ShapeDtypeStruct((m, D_MODEL), f32),
        scratch_shapes=[pltpu.VMEM((ROW_TILE, D_MODEL), bf16), pltpu.VMEM((ROW_TILE, D_MODEL), f32)],
        compiler_params=_cparams(1),
        name="outproj_mlp",
    )(x2d, o_fox, o_dil, o_moba, aux, gain, w_out, g2, w_up, w_down)


def _alibi_slopes():
    return jnp.exp2(-8.0 * jnp.arange(1, N_ALIBI + 1, dtype=f32) / N_ALIBI)


def _head_mean_matrix():
    r = jnp.arange(QK_CHUNK)[:, None] // HEAD_DIM
    c = jnp.arange(QK_CHUNK)[None, :] // HEAD_DIM
    return jnp.where(r == c, 1.0 / HEAD_DIM, 0.0).astype(bf16)


def _permute_families(a, axis):
    moba, fox, dil = jnp.split(a, [W_MOBA, W_MOBA + W_FOX], axis=axis)
    return jnp.concatenate([fox, dil, moba], axis=axis)


def kernel(x, attn_norm, w_in, b_forget, q_gain, k_gain, out_gain, w_out, mlp_norm, w_up, w_down):
    b, s, d = x.shape
    assert d == D_MODEL and s % ATT_TILE == 0 and (b * s) % ROW_TILE == 0
    depth = w_in.shape[0]
    slopes = _alibi_slopes()
    moba_lanes = jnp.concatenate(
        [_alibi_query_lanes(slopes[N_DIL:]), jnp.zeros((8 - N_MOBA, LANES), f32)])
    dil_lanes = _dilated_query_lanes(slopes[:N_DIL])
    gmat = _head_mean_matrix()
    x2d = x.reshape(b * s, d)
    for l in range(depth):
        wqkv, waux = _inproj_weights(w_in, l)
        baux = jnp.concatenate(
            [jnp.zeros((AUX_COLS - N_FOX,), f32), b_forget[l]]).reshape(1, AUX_COLS)
        gain = jnp.concatenate(
            [_permute_families(q_gain[l].reshape(1, d), 1) * (ATTN_SCALE * LOG2E),
             _permute_families(k_gain[l].reshape(1, d), 1)], axis=1)
        q, k, v, aux, *gathered = _inproj(x2d, attn_norm[l].reshape(1, d), wqkv, waux, baux, gain,
                                          gmat, s)
        q3, k3, v3 = (a.reshape(b, s, d) for a in (q, k, v))
        branches = [((q3.reshape(b, 1, s, d), k3.reshape(b, 1, s, d), v3.reshape(b, 1, s, d)),
                     OFF_DIL // LANES)]
        for j in range(len(DIL_GATHER)):
            branches.append((tuple(gathered[t * len(DIL_GATHER) + j] for t in range(3)), 0))
        cum = _fox_scan(aux.reshape(b, s, AUX_COLS))
        o_fox, o_moba = _fox_moba_attention(moba_lanes, q3, k3, v3, cum)
        o_dil = _dilated_attention(dil_lanes, branches)
        x2d = _outproj_mlp(x2d, o_fox.reshape(b * s, W_FOX), o_dil.reshape(b * s, W_DIL),
                           o_moba.reshape(b * s, W_MOBA), aux,
                           _permute_families(out_gain[l].reshape(1, d), 1),
                           _permute_families(w_out[l], 0).astype(bf16),
                           mlp_norm[l].reshape(1, d), w_up[l].astype(bf16), w_down[l].astype(bf16))
    return x2d.reshape(b, s, d)
```
